```python
import jax, jax.numpy as jnp
from jax import lax
import numpy as np

D_MODEL = 2048
BATCH = 4
SEQ = 8192
DEPTH = 1

CHUNK = 64
PLE_DIM = 256
EPS = 1e-6
H_A = 8
DK_A = 128
DV_A = 128
H_B = 4
DQK_B = 128
DV_B = 256
CONV_W = 4
M_INIT = -1e30
N_EXPERTS = 32
TOP_K = 4
D_FF = 2048
SWIGLU_LIMIT = 7.0
SWIGLU_ALPHA = 1.702
EXPERT_BLOCK = 256
IN_SPLIT_SIZES = (H_A * DK_A, H_A * DK_A, H_A * DV_A, H_A * DV_A,
                  2 * H_B * DQK_B, H_B * DV_B, H_B, H_B, H_B * DV_B,
                  D_MODEL, D_MODEL)
IN_COLS = sum(IN_SPLIT_SIZES)

kernel_name = 'hybrid_hgrn2_mlstm_moe_block'


def rmsnorm(x, g):
    xf = x.astype(jnp.float32)
    y = xf * lax.rsqrt(jnp.mean(xf * xf, axis=-1, keepdims=True) + EPS)
    return (y * g.astype(jnp.float32)).astype(x.dtype)


def head_rmsnorm(y, g, n_heads):
    b, s, w = y.shape
    yf = y.astype(jnp.float32).reshape(b, s, n_heads, w // n_heads)
    yf = yf * lax.rsqrt(jnp.mean(yf * yf, axis=-1, keepdims=True) + EPS)
    return (yf.reshape(b, s, w) * g.astype(jnp.float32)).astype(y.dtype)


def _split_cols(z):
    idx = [int(c) for c in np.cumsum(IN_SPLIT_SIZES)[:-1]]
    return jnp.split(z, idx, axis=-1)


def _causal_mask():
    return jnp.tril(jnp.ones((CHUNK, CHUNK), dtype=bool))


def _to_chunks(t):
    b, s, h, d = t.shape
    return t.reshape(b, s // CHUNK, CHUNK, h, d).transpose(0, 3, 1, 2, 4)


def _gate_chunks(t):
    b, s, h = t.shape
    return t.reshape(b, s // CHUNK, CHUNK, h).transpose(0, 3, 1, 2)


def _from_chunks(t):
    b, h, n, l, d = t.shape
    return t.transpose(0, 2, 3, 1, 4).reshape(b, n * l, h * d)


def causal_conv(u, w, b):
    s = u.shape[1]
    up = jnp.pad(u, ((0, 0), (CONV_W - 1, 0), (0, 0)))
    out = b
    for j in range(CONV_W):
        out = out + up[:, j:j + s] * w[j]
    return out


def hgrn2_mixer(q, f_logit, v, lb):
    dt = q.dtype
    b_, s_, _ = q.shape
    f32 = jnp.float32
    lbh = lb.astype(f32).reshape(H_A, DK_A)
    fl = f_logit.astype(f32).reshape(b_, s_, H_A, DK_A)
    log_f = jnp.log(lbh + (1.0 - lbh) * jax.nn.sigmoid(fl))
    k = (1.0 - lbh) * jax.nn.sigmoid(-fl)
    qf = jax.nn.silu(q.astype(f32)).reshape(b_, s_, H_A, DK_A)
    vf = v.astype(f32).reshape(b_, s_, H_A, DV_A)
    qf, k, vf, log_f = (_to_chunks(t) for t in (qf, k, vf, log_f))
    bc = jnp.cumsum(log_f, axis=3)
    b_ref = bc[:, :, :, CHUNK // 2:CHUNK // 2 + 1]
    b_end = bc[:, :, :, -1:]
    scores = jnp.einsum('bhnld,bhnmd->bhnlm', qf * jnp.exp(bc - b_ref), k * jnp.exp(b_ref - bc))
    scores = jnp.where(_causal_mask(), scores, 0.0)
    o = jnp.einsum('bhnlm,bhnme->bhnle', scores, vf)
    u = jnp.einsum('bhnld,bhnle->nbhde', k * jnp.exp(b_end - bc), vf)
    decay = jnp.exp(b_end[:, :, :, 0]).transpose(2, 0, 1, 3)

    def step(state, inp):
        a, du = inp
        return a[..., None] * state + du, state

    s0 = jnp.zeros((b_, H_A, DK_A, DV_A), f32)
    _, s_start = lax.scan(step, s0, (decay, u))
    o = o + jnp.einsum('bhnld,nbhde->bhnle', qf * jnp.exp(bc), s_start)
    return _from_chunks(o).astype(dt)


def mlstm_mixer(q, k, v, i_pre, f_pre):
    dt = v.dtype
    b_, s_, _ = q.shape
    f32 = jnp.float32
    qf = _to_chunks(q.astype(f32).reshape(b_, s_, H_B, DQK_B))
    kf = _to_chunks(k.astype(f32).reshape(b_, s_, H_B, DQK_B)) * (DQK_B ** -0.5)
    vf = _to_chunks(v.astype(f32).reshape(b_, s_, H_B, DV_B))
    ig = _gate_chunks(i_pre.astype(f32))
    fcum = jnp.cumsum(jax.nn.log_sigmoid(_gate_chunks(f_pre.astype(f32))), axis=-1)
    g_tot = fcum[..., -1]
    a_end = g_tot[..., None] - fcum + ig
    m_loc = a_end.max(axis=-1)
    w_end = jnp.exp(a_end - m_loc[..., None])
    c_loc = jnp.einsum('bhnl,bhnld,bhnle->nbhde', w_end, kf, vf)
    n_loc = jnp.einsum('bhnl,bhnld->nbhd', w_end, kf)

    def step(carry, inp):
        c, nv, m = carry
        g_n, m_l, c_l, n_l = inp
        m_new = jnp.maximum(g_n + m, m_l)
        s_prev = jnp.exp(g_n + m - m_new)
        s_loc = jnp.exp(m_l - m_new)
        c_new = s_prev[..., None, None] * c + s_loc[..., None, None] * c_l
        n_new = s_prev[..., None] * nv + s_loc[..., None] * n_l
        return (c_new, n_new, m_new), (c, nv, m)

    init = (jnp.zeros((b_, H_B, DQK_B, DV_B), f32), jnp.zeros((b_, H_B, DQK_B), f32),
            jnp.full((b_, H_B), M_INIT, f32))
    _, (c_st, n_st, m_st) = lax.scan(
        step, init, (g_tot.transpose(2, 0, 1), m_loc.transpose(2, 0, 1), c_loc, n_loc))
    d_log = fcum[..., :, None] - fcum[..., None, :] + ig[..., None, :]
    d_log = jnp.where(_causal_mask(), d_log, -jnp.inf)
    a_int = fcum + m_st.transpose(1, 2, 0)[..., None]
    m_t = jnp.maximum(d_log.max(axis=-1), a_int)
    w_in_chunk = jnp.exp(d_log - m_t[..., None])
    w_state = jnp.exp(a_int - m_t)
    qk = jnp.einsum('bhnld,bhnmd->bhnlm', qf, kf) * w_in_chunk
    num = (jnp.einsum('bhnlm,bhnme->bhnle', qk, vf)
           + w_state[..., None] * jnp.einsum('bhnld,nbhde->bhnle', qf, c_st))
    den = qk.sum(axis=-1) + w_state * jnp.einsum('bhnld,nbhd->bhnl', qf, n_st)
    h = num / jnp.maximum(jnp.abs(den), jnp.exp(-m_t))[..., None]
    return _from_chunks(h).astype(dt)


def moe(h, router_w, router_b, w_gu, b_gu, w_dn, b_dn):
    bsz, s, d = h.shape
    hf = h.reshape(-1, d)
    t = hf.shape[0]
    logits = hf.astype(jnp.float32) @ router_w.astype(jnp.float32) + router_b.astype(jnp.float32)
    top_val, top_idx = lax.top_k(logits, TOP_K)
    gates = jax.nn.softmax(top_val, axis=-1).astype(h.dtype)
    flat_e = top_idx.reshape(-1)
    flat_tok = jnp.repeat(jnp.arange(t, dtype=jnp.int32), TOP_K)
    order = jnp.argsort(flat_e)
    e_sorted = flat_e[order]
    counts = jnp.bincount(flat_e, length=N_EXPERTS)
    padded = (counts + EXPERT_BLOCK - 1) // EXPERT_BLOCK * EXPERT_BLOCK
    start = jnp.cumsum(counts) - counts
    pad_end = jnp.cumsum(padded)
    pad_start = pad_end - padded
    dest = pad_start[e_sorted] + jnp.arange(t * TOP_K, dtype=jnp.int32) - start[e_sorted]
    n_rows = t * TOP_K + N_EXPERTS * EXPERT_BLOCK
    n_blocks = n_rows // EXPERT_BLOCK
    row_tok = jnp.zeros((n_rows,), jnp.int32).at[dest].set(flat_tok[order])
    row_gate = jnp.zeros((n_rows,), h.dtype).at[dest].set(gates.reshape(-1)[order])
    block_expert = jnp.minimum(
        jnp.searchsorted(pad_end, jnp.arange(n_blocks, dtype=jnp.int32) * EXPERT_BLOCK, side='right'),
        N_EXPERTS - 1)

    def expert_block(args):
        e, tok, g = args
        xb = hf[tok]
        gu = xb @ w_gu[e] + b_gu[e]
        a, u = jnp.split(gu, 2, axis=-1)
        a = jnp.minimum(a, SWIGLU_LIMIT)
        u = jnp.clip(u, -SWIGLU_LIMIT, SWIGLU_LIMIT)
        y = ((u + 1.0) * a * jax.nn.sigmoid(SWIGLU_ALPHA * a)) @ w_dn[e] + b_dn[e]
        return y * g[:, None]

    y_rows = lax.map(expert_block, (block_expert,
                                    row_tok.reshape(n_blocks, EXPERT_BLOCK),
                                    row_gate.reshape(n_blocks, EXPERT_BLOCK)))
    out = jnp.zeros_like(hf).at[row_tok].add(y_rows.reshape(n_rows, d))
    return out.reshape(bsz, s, d)


def setup_inputs(seed: int = 0) -> dict:
    key = jax.random.key(seed)
    ks = jax.random.split(key, 26)
    f32 = jnp.float32

    def nrm(k, shape, scale):
        return jax.random.normal(k, shape, f32) * scale

    def gain(k, shape):
        return 1.0 + 0.02 * jax.random.normal(k, shape, f32)

    L = DEPTH
    return {
        'x': nrm(ks[0], (BATCH, SEQ, D_MODEL), 1.0),
        'p': nrm(ks[1], (DEPTH, BATCH, SEQ, PLE_DIM), 1.0),
        'norm_mix': gain(ks[2], (L, D_MODEL)),
        'w_in': nrm(ks[3], (L, D_MODEL, IN_COLS), D_MODEL ** -0.5),
        'hgrn_lb': nrm(ks[4], (DEPTH + 1, H_A * DK_A), 0.3),
        'hgrn_norm': gain(ks[5], (L, H_A * DV_A)),
        'mlstm_conv_w': nrm(ks[6], (L, CONV_W, 2 * H_B * DQK_B), CONV_W ** -0.5),
        'mlstm_conv_b': nrm(ks[7], (L, 2 * H_B * DQK_B), 0.02),
        'mlstm_b_i': nrm(ks[8], (L, H_B), 0.1),
        'mlstm_b_f': 3.0 + nrm(ks[9], (L, H_B), 0.1),
        'mlstm_norm': gain(ks[10], (L, H_B * DV_B)),
        'w_branch_a': nrm(ks[11], (L, H_A * DV_A, D_MODEL), (H_A * DV_A) ** -0.5),
        'w_branch_b': nrm(ks[12], (L, H_B * DV_B, D_MODEL), (H_B * DV_B) ** -0.5),
        'w_out': nrm(ks[13], (L, D_MODEL, D_MODEL), D_MODEL ** -0.5),
        'norm_moe': gain(ks[14], (L, D_MODEL)),
        'router_w': nrm(ks[15], (L, D_MODEL, N_EXPERTS), D_MODEL ** -0.5),
        'router_b': nrm(ks[16], (L, N_EXPERTS), 0.01),
        'exp_w_gu': nrm(ks[17], (L, N_EXPERTS, D_MODEL, 2 * D_FF), D_MODEL ** -0.5),
        'exp_b_gu': nrm(ks[18], (L, N_EXPERTS, 2 * D_FF), 0.01),
        'exp_w_dn': nrm(ks[19], (L, N_EXPERTS, D_FF, D_MODEL), D_FF ** -0.5),
        'exp_b_dn': nrm(ks[20], (L, N_EXPERTS, D_MODEL), 0.01),
        'norm_ple': gain(ks[21], (L, D_MODEL)),
        'w_ple': nrm(ks[22], (L, PLE_DIM, D_MODEL), PLE_DIM ** -0.5),
        'w_ple_gate': nrm(ks[23], (L, D_MODEL, D_MODEL), D_MODEL ** -0.5),
        'norm_final': gain(ks[24], (D_MODEL,)),
    }


def reference(x, p, norm_mix, w_in, hgrn_lb, hgrn_norm, mlstm_conv_w, mlstm_conv_b,
              mlstm_b_i, mlstm_b_f, mlstm_norm, w_branch_a, w_branch_b, w_out, norm_moe,
              router_w, router_b, exp_w_gu, exp_b_gu, exp_w_dn, exp_b_dn, norm_ple,
              w_ple, w_ple_gate, norm_final):
    lower_bounds = jnp.cumsum(jax.nn.softmax(hgrn_lb.astype(jnp.float32), axis=0), axis=0)
    for i in range(DEPTH):
        h = rmsnorm(x, norm_mix[i])
        (a_q, a_f, a_v, a_g, b_qk, b_v, b_i, b_f, b_o, r_a, r_b) = _split_cols(h @ w_in[i])
        y_a = hgrn2_mixer(a_q, a_f, a_v, lower_bounds[i])
        y_a = head_rmsnorm(y_a, hgrn_norm[i], H_A) * jax.nn.silu(a_g)
        qk = jax.nn.silu(causal_conv(b_qk, mlstm_conv_w[i], mlstm_conv_b[i]))
        b_q, b_k = jnp.split(qk, 2, axis=-1)
        y_b = mlstm_mixer(b_q, b_k, b_v, b_i + mlstm_b_i[i], b_f + mlstm_b_f[i])
        y_b = head_rmsnorm(y_b, mlstm_norm[i], H_B) * jax.nn.sigmoid(b_o)
        mix = (jax.nn.sigmoid(r_a) * (y_a @ w_branch_a[i])
               + jax.nn.sigmoid(r_b) * (y_b @ w_branch_b[i]))
        x = x + mix @ w_out[i]
        x = x + moe(rmsnorm(x, norm_moe[i]), router_w[i], router_b[i],
                    exp_w_gu[i], exp_b_gu[i], exp_w_dn[i], exp_b_dn[i])
        x = x + (p[i] @ w_ple[i]) * jax.nn.sigmoid(rmsnorm(x, norm_ple[i]) @ w_ple_gate[i])
    return rmsnorm(x, norm_final)
```

```python
import functools

import jax
import jax.numpy as jnp
from jax import lax
from jax.experimental import pallas as pl
from jax.experimental.pallas import tpu as pltpu

F32 = jnp.float32
BF16 = jnp.bfloat16
U32 = jnp.uint32
I32 = jnp.int32

EPS = 1e-6
CHUNK = 64
H_A, DK_A, DV_A = 8, 128, 128
H_B, DQK_B, DV_B = 4, 128, 256
CONV_W = 4
M_INIT = -1e30
N_EXPERTS = 32
TOP_K = 4
SWIGLU_LIMIT = 7.0
SWIGLU_ALPHA = 1.702

V7X_VMEM_LIMIT_BYTES = 56 * 1024 * 1024
LANES = 128

EXPERT_TILE = 512
FF_CHUNK = 512


def _cparams(sem):
    return pltpu.CompilerParams(dimension_semantics=sem, vmem_limit_bytes=V7X_VMEM_LIMIT_BYTES)


def _sigmoid(x):
    return jax.nn.sigmoid(x)


def _split3(x):
    hi = x.astype(BF16)
    r1 = x - hi.astype(F32)
    mid = r1.astype(BF16)
    lo = (r1 - mid.astype(F32)).astype(BF16)
    return hi, mid, lo


def _cumsum_rows(tril, x):
    hi, mid, lo = _split3(x)
    acc = jnp.dot(tril, hi, preferred_element_type=F32)
    acc = acc + jnp.dot(tril, mid, preferred_element_type=F32)
    return acc + jnp.dot(tril, lo, preferred_element_type=F32)


def _dot_tn(a, b):
    return lax.dot_general(a, b, (((0,), (0,)), ((), ())), preferred_element_type=F32)


def _dot_nt(a, b):
    return lax.dot_general(a, b, (((1,), (1,)), ((), ())), preferred_element_type=F32)


def _inproj_kernel(x_ref, g_ref, w_ref, wg_ref, z_ref, gc_ref, h_scr):
    @pl.when(pl.program_id(1) == 0)
    def _():
        x = x_ref[...]
        ms = jnp.mean(x * x, axis=-1, keepdims=True)
        hb = (x * lax.rsqrt(ms + EPS) * g_ref[...]).astype(BF16)
        h_scr[...] = hb
        gc_ref[...] = jnp.dot(hb, wg_ref[...], preferred_element_type=F32)

    z_ref[...] = jnp.dot(h_scr[...], w_ref[...], preferred_element_type=F32).astype(BF16)


def _inproj(x2d, gain, w_main, w_gate, tm, tn):
    t, d = x2d.shape
    n = w_main.shape[1]
    return pl.pallas_call(
        _inproj_kernel,
        out_shape=(jax.ShapeDtypeStruct((t, n), BF16), jax.ShapeDtypeStruct((t, LANES), F32)),
        grid=(t // tm, n // tn),
        in_specs=[
            pl.BlockSpec((tm, d), lambda i, j: (i, 0)),
            pl.BlockSpec((1, d), lambda i, j: (0, 0)),
            pl.BlockSpec((d, tn), lambda i, j: (0, j)),
            pl.BlockSpec((d, LANES), lambda i, j: (0, 0)),
        ],
        out_specs=(
            pl.BlockSpec((tm, tn), lambda i, j: (i, j)),
            pl.BlockSpec((tm, LANES), lambda i, j: (i, 0)),
        ),
        scratch_shapes=[pltpu.VMEM((tm, d), BF16)],
        compiler_params=_cparams(("parallel", "arbitrary")),
        name="inproj",
    )(x2d, gain, w_main, w_gate)


def _hgrn_kernel(q_ref, f_ref, v_ref, g_ref, lb_ref, gain_ref, tril_ref, o_ref,
                 st_ref, qd_s, kd_s, ke_s, qe_s):
    @pl.when(pl.program_id(1) == 0)
    def _():
        st_ref[...] = jnp.zeros_like(st_ref)

    lbr = lb_ref[...]
    mx = jnp.maximum(lbr[0:1], lbr[1:2])
    e0 = jnp.exp(lbr[0:1] - mx)
    e1 = jnp.exp(lbr[1:2] - mx)
    lb = e0 / (e0 + e1)
    one_m_lb = 1.0 - lb
    tril = tril_ref[...]
    gain = gain_ref[...]
    rows = lax.broadcasted_iota(I32, (CHUNK, CHUNK), 0)
    cols = lax.broadcasted_iota(I32, (CHUNK, CHUNK), 1)
    causal = rows >= cols
    n_chunks = q_ref.shape[0] // CHUNK

    def chunk(c, carry):
        r0 = pl.multiple_of(c * CHUNK, CHUNK)
        q = q_ref[pl.ds(r0, CHUNK), :].astype(F32)
        fl = f_ref[pl.ds(r0, CHUNK), :].astype(F32)
        log_f = jnp.log(lb + one_m_lb * _sigmoid(fl))
        k = one_m_lb * _sigmoid(-fl)
        qs = q * _sigmoid(q)
        bc = _cumsum_rows(tril, log_f)
        b_ref = bc[CHUNK // 2:CHUNK // 2 + 1, :]
        b_end = bc[CHUNK - 1:CHUNK, :]
        qd_s[...] = (qs * jnp.exp(bc - b_ref)).astype(BF16)
        kd_s[...] = (k * jnp.exp(b_ref - bc)).astype(BF16)
        ke_s[...] = (k * jnp.exp(b_end - bc)).astype(BF16)
        qe_s[...] = (qs * jnp.exp(bc)).astype(BF16)
        decay = jnp.exp(b_end)
        for h in range(H_A):
            sl = slice(h * DK_A, (h + 1) * DK_A)
            scores = _dot_nt(qd_s[:, sl], kd_s[:, sl])
            scores = jnp.where(causal, scores, 0.0).astype(BF16)
            vh = v_ref[pl.ds(r0, CHUNK), sl]
            st = st_ref[h]
            o = jnp.dot(scores, vh, preferred_element_type=F32)
            o = o + _dot_nt(qe_s[:, sl], st.astype(BF16))
            u_t = _dot_tn(vh, ke_s[:, sl])
            st_ref[h] = st * decay[:, sl] + u_t
            ms = jnp.mean(o * o, axis=-1, keepdims=True)
            gt = g_ref[pl.ds(r0, CHUNK), sl].astype(F32)
            y = o * lax.rsqrt(ms + EPS) * gain[:, sl] * (gt * _sigmoid(gt))
            o_ref[pl.ds(r0, CHUNK), sl] = y.astype(BF16)
        return carry

    lax.fori_loop(0, n_chunks, chunk, 0)


def _hgrn(z3, lb_raw, gain, tril, ts):
    b, s, _ = z3.shape
    w = H_A * DK_A

    def zspec(col):
        return pl.BlockSpec((None, ts, w), lambda i, j, col=col: (i, j, col))

    return pl.pallas_call(
        _hgrn_kernel,
        out_shape=jax.ShapeDtypeStruct((b, s, w), BF16),
        grid=(b, s // ts),
        in_specs=[zspec(0), zspec(1), zspec(2), zspec(3),
                  pl.BlockSpec((2, w), lambda i, j: (0, 0)),
                  pl.BlockSpec((1, w), lambda i, j: (0, 0)),
                  pl.BlockSpec((CHUNK, CHUNK), lambda i, j: (0, 0))],
        out_specs=pl.BlockSpec((None, ts, w), lambda i, j: (i, j, 0)),
        scratch_shapes=[pltpu.VMEM((H_A, DV_A, DK_A), F32),
                        pltpu.VMEM((CHUNK, w), BF16), pltpu.VMEM((CHUNK, w), BF16),
                        pltpu.VMEM((CHUNK, w), BF16), pltpu.VMEM((CHUNK, w), BF16)],
        compiler_params=_cparams(("parallel", "arbitrary")),
        name="hgrn2",
    )(z3, z3, z3, z3, lb_raw, gain, tril)


def _log_sigmoid(x):
    return jnp.minimum(x, 0.0) - jnp.log1p(jnp.exp(-jnp.abs(x)))


def _mlstm_kernel(qk_ref, v_ref, o_in_ref, gc_ref, cw_ref, cb_ref, gb_ref, gain_ref, tril_ref,
                  o_ref, c_ref, n_ref, m_ref, tail_ref, ext_s):
    @pl.when(pl.program_id(1) == 0)
    def _():
        c_ref[...] = jnp.zeros_like(c_ref)
        n_ref[...] = jnp.zeros_like(n_ref)
        m_ref[...] = jnp.full_like(m_ref, M_INIT)
        tail_ref[...] = jnp.zeros_like(tail_ref)

    tril = tril_ref[...]
    gain = gain_ref[...]
    cw = cw_ref[...]
    cb = cb_ref[...]
    gb = gb_ref[...]
    rows = lax.broadcasted_iota(I32, (CHUNK, CHUNK), 0)
    cols = lax.broadcasted_iota(I32, (CHUNK, CHUNK), 1)
    causal = rows >= cols
    n_chunks = qk_ref.shape[0] // CHUNK
    hq = H_B * DQK_B
    scale = DQK_B ** -0.5

    def chunk(c, carry):
        r0 = pl.multiple_of(c * CHUNK, CHUNK)
        u = qk_ref[pl.ds(r0, CHUNK), :].astype(F32)
        ext_s[0:8, :] = tail_ref[...]
        ext_s[8:8 + CHUNK, :] = u
        tail_ref[...] = u[CHUNK - 8:CHUNK, :]
        conv = cb + cw[CONV_W - 1:CONV_W, :] * u
        for j in range(CONV_W - 1):
            lag = CONV_W - 1 - j
            conv = conv + cw[j:j + 1, :] * ext_s[8 - lag:8 - lag + CHUNK, :]
        qk = conv * _sigmoid(conv)
        q = qk[:, :hq]
        k = qk[:, hq:] * scale

        g = gc_ref[pl.ds(r0, CHUNK), :] + gb
        fc = _cumsum_rows(tril, _log_sigmoid(g))
        zpad = jnp.zeros((LANES - CHUNK, LANES), F32)
        g_t = jnp.concatenate([g, zpad], axis=0).T
        fc_t = jnp.concatenate([fc, zpad], axis=0).T

        for h in range(H_B):
            ig_c = g[:, h:h + 1]
            fc_c = fc[:, H_B + h:H_B + h + 1]
            ig_r = g_t[h:h + 1, 0:CHUNK]
            fc_r = fc_t[H_B + h:H_B + h + 1, 0:CHUNK]
            g_tot = fc_c[CHUNK - 1:CHUNK, :]
            qh = q[:, h * DQK_B:(h + 1) * DQK_B]
            kh = k[:, h * DQK_B:(h + 1) * DQK_B]
            vh = v_ref[pl.ds(r0, CHUNK), h * DV_B:(h + 1) * DV_B]
            qb = qh.astype(BF16)
            c_st = c_ref[h]
            n_st = n_ref[h]
            m_st = m_ref[h][0:1, 0:1]

            d_log = jnp.where(causal, fc_c - fc_r + ig_r, -jnp.inf)
            a_int = fc_c + m_st
            m_t = jnp.maximum(jnp.max(d_log, axis=-1, keepdims=True), a_int)
            w_in = jnp.exp(d_log - m_t)
            w_state = jnp.exp(a_int - m_t)
            qkw = _dot_nt(qb, kh.astype(BF16)) * w_in
            num = (jnp.dot(qkw.astype(BF16), vh, preferred_element_type=F32)
                   + w_state * jnp.dot(qb, c_st.astype(BF16), preferred_element_type=F32))
            den = (jnp.sum(qkw, axis=-1, keepdims=True)
                   + w_state * jnp.sum(qh * n_st, axis=-1, keepdims=True))
            hh = num / jnp.maximum(jnp.abs(den), jnp.exp(-m_t))

            a_end = g_tot - fc_c + ig_c
            m_loc = jnp.max(a_end, axis=0, keepdims=True)
            kw = kh * jnp.exp(a_end - m_loc)
            c_loc = _dot_tn(kw.astype(BF16), vh)
            n_loc = jnp.sum(kw, axis=0, keepdims=True)
            m_new = jnp.maximum(g_tot + m_st, m_loc)
            s_prev = jnp.exp(g_tot + m_st - m_new)
            s_loc = jnp.exp(m_loc - m_new)
            c_ref[h] = s_prev * c_st + s_loc * c_loc
            n_ref[h] = s_prev * n_st + s_loc * n_loc
            m_ref[h] = jnp.broadcast_to(m_new, m_ref.shape[1:])

            vs = slice(h * DV_B, (h + 1) * DV_B)
            ms = jnp.mean(hh * hh, axis=-1, keepdims=True)
            og = o_in_ref[pl.ds(r0, CHUNK), vs].astype(F32)
            y = hh * lax.rsqrt(ms + EPS) * gain[:, vs] * _sigmoid(og)
            o_ref[pl.ds(r0, CHUNK), vs] = y.astype(BF16)
        return carry

    lax.fori_loop(0, n_chunks, chunk, 0)


def _mlstm(z3, gc3, conv_w, conv_b, gate_bias, gain, tril, ts):
    b, s, _ = z3.shape
    w = H_B * DV_B

    def zspec(col):
        return pl.BlockSpec((None, ts, w), lambda i, j, col=col: (i, j, col))

    def const(shape):
        return pl.BlockSpec(shape, lambda i, j: (0,) * len(shape))

    return pl.pallas_call(
        _mlstm_kernel,
        out_shape=jax.ShapeDtypeStruct((b, s, w), BF16),
        grid=(b, s // ts),
        in_specs=[zspec(4), zspec(5), zspec(6),
                  pl.BlockSpec((None, ts, LANES), lambda i, j: (i, j, 0)),
                  const((CONV_W, w)), const((1, w)), const((1, LANES)), const((1, w)),
                  const((CHUNK, CHUNK))],
        out_specs=pl.BlockSpec((None, ts, w), lambda i, j: (i, j, 0)),
        scratch_shapes=[pltpu.VMEM((H_B, DQK_B, DV_B), F32),
                        pltpu.VMEM((H_B, 1, DQK_B), F32),
                        pltpu.VMEM((H_B, 8, LANES), F32),
                        pltpu.VMEM((8, w), F32),
                        pltpu.VMEM((8 + CHUNK, w), F32)],
        compiler_params=_cparams(("parallel", "arbitrary")),
        name="mlstm",
    )(z3, z3, z3, gc3, conv_w, conv_b, gate_bias, gain, tril)


def _merge_kernel(ya_ref, yb_ref, ra0_ref, ra1_ref, rb0_ref, rb1_ref, x_ref,
                  wa_ref, wb_ref, wo_ref, g_ref, rw_ref, rb_ref, upper_ref,
                  x1_ref, hp_ref, idx_ref, gate_ref, gcol_ref, rank_ref, cnt_ref, carry_ref):
    @pl.when(pl.program_id(0) == 0)
    def _():
        carry_ref[...] = jnp.zeros_like(carry_ref)

    half = wo_ref.shape[0] // 2
    pa = jnp.dot(ya_ref[...], wa_ref[...], preferred_element_type=F32)
    pb = jnp.dot(yb_ref[...], wb_ref[...], preferred_element_type=F32)
    mix0 = (_sigmoid(ra0_ref[...].astype(F32)) * pa[:, :half]
            + _sigmoid(rb0_ref[...].astype(F32)) * pb[:, :half]).astype(BF16)
    mix1 = (_sigmoid(ra1_ref[...].astype(F32)) * pa[:, half:]
            + _sigmoid(rb1_ref[...].astype(F32)) * pb[:, half:]).astype(BF16)
    x1 = (x_ref[...] + jnp.dot(mix0, wo_ref[0:half, :], preferred_element_type=F32)
          + jnp.dot(mix1, wo_ref[half:, :], preferred_element_type=F32))
    x1_ref[...] = x1

    ms = jnp.mean(x1 * x1, axis=-1, keepdims=True)
    h2 = x1 * lax.rsqrt(ms + EPS) * g_ref[...]
    hb = h2.astype(BF16)
    hp_ref[...] = h2

    h_lo = (h2 - hb.astype(F32)).astype(BF16)
    rw = rw_ref[...]
    rw_hi = rw.astype(BF16)
    rw_lo = (rw - rw_hi.astype(F32)).astype(BF16)
    logits = (_dot_nt(rw_hi, hb) + _dot_nt(rw_hi, h_lo) + _dot_nt(rw_lo, hb)) + rb_ref[...]

    tm = logits.shape[1]
    erow = lax.broadcasted_iota(I32, (N_EXPERTS, tm), 0)
    work = logits
    vals, idxs = [], []
    for _ in range(TOP_K):
        mval = jnp.max(work, axis=0, keepdims=True)
        idx = jnp.min(jnp.where(work == mval, erow, N_EXPERTS), axis=0, keepdims=True)
        work = jnp.where(erow == idx, -jnp.inf, work)
        vals.append(mval)
        idxs.append(idx)
    exps = [jnp.exp(v - vals[0]) for v in vals]
    denom = exps[0] + exps[1] + exps[2] + exps[3]
    gates = [e / denom for e in exps]

    onehots = [(erow == idx).astype(F32) for idx in idxs]
    mask = onehots[0] + onehots[1] + onehots[2] + onehots[3]
    before = jnp.dot(mask.astype(BF16), upper_ref[...], preferred_element_type=F32)
    before = before + carry_ref[:, 0:1]
    carry = carry_ref[...] + jnp.sum(mask, axis=1, keepdims=True)
    carry_ref[...] = carry
    cnt_ref[...] = carry

    idx_ref[...] = jnp.concatenate(idxs, axis=0)
    gate_rows = jnp.concatenate(gates, axis=0)
    gate_ref[...] = gate_rows
    rank_ref[...] = jnp.concatenate(
        [jnp.sum(oh * before, axis=0, keepdims=True) for oh in onehots], axis=0).astype(I32)
    padded = jnp.concatenate([gate_rows, jnp.zeros((LANES - TOP_K, tm), F32)], axis=0)
    gcol_ref[...] = padded.T


def _merge(ya, yb, z, x2d, wa, wb, wo, gain, rw_t, rb_col, upper, tm):
    t, d = x2d.shape
    half = d // 2
    ra_blk = (4 * H_A * DK_A + 3 * H_B * DV_B) // half

    def rows(width, col=0):
        return pl.BlockSpec((tm, width), lambda i, col=col: (i, col))

    def const(shape):
        return pl.BlockSpec(shape, lambda i: (0,) * len(shape), pipeline_mode=pl.Buffered(1))

    out_shape = (
        jax.ShapeDtypeStruct((t, d), F32),
        jax.ShapeDtypeStruct((t, d), F32),
        jax.ShapeDtypeStruct((TOP_K, t), I32),
        jax.ShapeDtypeStruct((TOP_K, t), F32),
        jax.ShapeDtypeStruct((t, LANES), F32),
        jax.ShapeDtypeStruct((TOP_K, t), I32),
        jax.ShapeDtypeStruct((N_EXPERTS, LANES), F32),
    )
    return pl.pallas_call(
        _merge_kernel,
        out_shape=out_shape,
        grid=(t // tm,),
        in_specs=[rows(ya.shape[1]), rows(yb.shape[1]),
                  rows(half, ra_blk), rows(half, ra_blk + 1), rows(half, ra_blk + 2),
                  rows(half, ra_blk + 3), rows(d),
                  const(wa.shape), const(wb.shape), const(wo.shape), const((1, d)),
                  const(rw_t.shape), const((N_EXPERTS, 1)), const((tm, tm))],
        out_specs=(rows(d), rows(d),
                   pl.BlockSpec((TOP_K, tm), lambda i: (0, i)),
                   pl.BlockSpec((TOP_K, tm), lambda i: (0, i)),
                   rows(LANES),
                   pl.BlockSpec((TOP_K, tm), lambda i: (0, i)),
                   pl.BlockSpec((N_EXPERTS, LANES), lambda i: (0, 0))),
        scratch_shapes=[pltpu.VMEM((N_EXPERTS, LANES), F32)],
        compiler_params=_cparams(("arbitrary",)),
        name="merge_router",
    )(ya, yb, z, z, z, z, x2d, wa, wb, wo, gain, rw_t, rb_col, upper)


def _dispatch_kernel(pos_ref, src_ref, dst_in_ref, dst_ref, sem):
    del dst_in_ref
    td = pos_ref.shape[1]
    base = pl.program_id(0) * td

    def row_copy(r, k):
        return pltpu.make_async_copy(src_ref.at[pl.ds(base + r, 1)],
                                     dst_ref.at[pl.ds(pos_ref[k, r], 1)], sem)

    def issue(r, carry):
        for k in range(TOP_K):
            row_copy(r, k).start()
        return carry

    def drain(r, carry):
        for k in range(TOP_K):
            row_copy(r, k).wait()
        return carry

    lax.fori_loop(0, td, issue, 0)
    lax.fori_loop(0, td, drain, 0)


def _dispatch(pos, src, dst_init, td):
    t = src.shape[0]
    return pl.pallas_call(
        _dispatch_kernel,
        out_shape=jax.ShapeDtypeStruct(dst_init.shape, dst_init.dtype),
        grid=(t // td,),
        in_specs=[pl.BlockSpec((TOP_K, td), lambda i: (0, i), memory_space=pltpu.SMEM),
                  pl.BlockSpec(memory_space=pl.ANY),
                  pl.BlockSpec(memory_space=pl.ANY)],
        out_specs=pl.BlockSpec(memory_space=pl.ANY),
        scratch_shapes=[pltpu.SemaphoreType.DMA],
        input_output_aliases={2: 0},
        compiler_params=pltpu.CompilerParams(dimension_semantics=("arbitrary",),
                                             has_side_effects=True),
        name="dispatch",
    )(pos, src, dst_init)


def _expert_kernel(te_ref, tv_ref, xs_ref, wa_ref, wu_ref, ba_ref, bu_ref, wd_ref, bd_ref,
                   o_ref, xb_s):
    i = pl.program_id(0)
    c = pl.program_id(1)
    valid = tv_ref[i] > 0

    @pl.when(jnp.logical_and(valid, c == 0))
    def _():
        xb_s[...] = xs_ref[...].astype(BF16)

    @pl.when(valid)
    def _():
        xb = xb_s[...]
        a = jnp.dot(xb, wa_ref[...], preferred_element_type=F32) + ba_ref[...]
        u = jnp.dot(xb, wu_ref[...], preferred_element_type=F32) + bu_ref[...]
        a = jnp.minimum(a, SWIGLU_LIMIT)
        u = jnp.clip(u, -SWIGLU_LIMIT, SWIGLU_LIMIT)
        act = ((u + 1.0) * a * _sigmoid(SWIGLU_ALPHA * a)).astype(BF16)
        y = jnp.dot(act, wd_ref[...], preferred_element_type=F32)

        @pl.when(c == 0)
        def _():
            o_ref[...] = y + bd_ref[...]

        @pl.when(c != 0)
        def _():
            o_ref[...] += y

    @pl.when(jnp.logical_and(jnp.logical_not(valid), c == 0))
    def _():
        o_ref[...] = jnp.zeros_like(o_ref)


def _experts(tile_expert, tile_valid, xs, w_gu, b_gu, w_dn, b_dn):
    n_rows, d = xs.shape
    d_ff = w_dn.shape[1]
    n_tiles = n_rows // EXPERT_TILE
    n_c = d_ff // FF_CHUNK
    last = n_c - 1

    def cc(c, tv, i):
        return jnp.where(tv[i] > 0, c, last)

    grid_spec = pltpu.PrefetchScalarGridSpec(
        num_scalar_prefetch=2,
        grid=(n_tiles, n_c),
        in_specs=[
            pl.BlockSpec((EXPERT_TILE, d), lambda i, c, te, tv: (i, 0)),
            pl.BlockSpec((None, d, FF_CHUNK), lambda i, c, te, tv: (te[i], 0, cc(c, tv, i))),
            pl.BlockSpec((None, d, FF_CHUNK), lambda i, c, te, tv: (te[i], 0, n_c + cc(c, tv, i))),
            pl.BlockSpec((None, 1, FF_CHUNK), lambda i, c, te, tv: (te[i], 0, cc(c, tv, i))),
            pl.BlockSpec((None, 1, FF_CHUNK), lambda i, c, te, tv: (te[i], 0, n_c + cc(c, tv, i))),
            pl.BlockSpec((None, FF_CHUNK, d), lambda i, c, te, tv: (te[i], cc(c, tv, i), 0)),
            pl.BlockSpec((None, 1, d), lambda i, c, te, tv: (te[i], 0, 0)),
        ],
        out_specs=pl.BlockSpec((EXPERT_TILE, d), lambda i, c, te, tv: (i, 0)),
        scratch_shapes=[pltpu.VMEM((EXPERT_TILE, d), BF16)],
    )
    return pl.pallas_call(
        _expert_kernel,
        out_shape=jax.ShapeDtypeStruct((n_rows, d), F32),
        grid_spec=grid_spec,
        compiler_params=_cparams(("arbitrary", "arbitrary")),
        name="experts",
    )(tile_expert, tile_valid, xs, w_gu, w_gu, b_gu, b_gu, w_dn, b_dn)


def _combine_kernel(pos_ref, ys_ref, x1_ref, gcol_ref, p_ref, g_ple_ref, wpg_ref, wple_ref,
                    g_fin_ref, o_ref, buf, sem):
    tc = x1_ref.shape[0]

    def row_copy(r, k):
        return pltpu.make_async_copy(ys_ref.at[pl.ds(pos_ref[k, r], 1)],
                                     buf.at[k, pl.ds(r, 1)], sem)

    def issue(r, carry):
        for k in range(TOP_K):
            row_copy(r, k).start()
        return carry

    def drain(r, carry):
        for k in range(TOP_K):
            row_copy(r, k).wait()
        return carry

    lax.fori_loop(0, tc, issue, 0)
    ple = jnp.dot(p_ref[...].astype(BF16), wple_ref[...], preferred_element_type=F32)
    lax.fori_loop(0, tc, drain, 0)

    gcol = gcol_ref[...]
    moe = buf[0] * gcol[:, 0:1]
    for k in range(1, TOP_K):
        moe = moe + buf[k] * gcol[:, k:k + 1]
    x2 = x1_ref[...] + moe
    ms = jnp.mean(x2 * x2, axis=-1, keepdims=True)
    h3 = (x2 * lax.rsqrt(ms + EPS) * g_ple_ref[...]).astype(BF16)
    gate = _sigmoid(jnp.dot(h3, wpg_ref[...], preferred_element_type=F32))
    x3 = x2 + ple * gate
    ms3 = jnp.mean(x3 * x3, axis=-1, keepdims=True)
    o_ref[...] = x3 * lax.rsqrt(ms3 + EPS) * g_fin_ref[...]


def _combine(pos, ys, x1, gcol, p2d, g_ple, wpg, wple, g_fin, tc):
    t, d = x1.shape

    def rows(width):
        return pl.BlockSpec((tc, width), lambda i: (i, 0))

    def const(shape):
        return pl.BlockSpec(shape, lambda i: (0,) * len(shape), pipeline_mode=pl.Buffered(1))

    return pl.pallas_call(
        _combine_kernel,
        out_shape=jax.ShapeDtypeStruct((t, d), F32),
        grid=(t // tc,),
        in_specs=[pl.BlockSpec((TOP_K, tc), lambda i: (0, i), memory_space=pltpu.SMEM),
                  pl.BlockSpec(memory_space=pl.ANY),
                  rows(d), rows(LANES), rows(p2d.shape[1]),
                  const((1, d)), const(wpg.shape), const(wple.shape), const((1, d))],
        out_specs=rows(d),
        scratch_shapes=[pltpu.VMEM((TOP_K, tc, d), F32), pltpu.SemaphoreType.DMA],
        compiler_params=_cparams(("arbitrary",)),
        name="combine_ple",
    )(pos, ys, x1, gcol, p2d, g_ple, wpg, wple, g_fin)


def _pick(n, pref):
    return pref if n % pref == 0 else n


def kernel(x, p, norm_mix, w_in, hgrn_lb, hgrn_norm, mlstm_conv_w, mlstm_conv_b, mlstm_b_i,
           mlstm_b_f, mlstm_norm, w_branch_a, w_branch_b, w_out, norm_moe, router_w, router_b,
           exp_w_gu, exp_b_gu, exp_w_dn, exp_b_dn, norm_ple, w_ple, w_ple_gate, norm_final):
    b, s, d = x.shape
    t = b * s
    depth = norm_mix.shape[0]
    wa_cols = H_A * DK_A
    wb_cols = H_B * DV_B
    g0 = 4 * wa_cols + 2 * wb_cols
    g1 = g0 + 2 * H_B
    tril = jnp.tril(jnp.ones((CHUNK, CHUNK), F32)).astype(BF16)
    x2d = x.reshape(t, d)

    for i in range(depth):
        w_i = w_in[i]
        w_main = jnp.concatenate([w_i[:, :g0], w_i[:, g1:]], axis=1).astype(BF16)
        w_gate = jnp.pad(w_i[:, g0:g1], ((0, 0), (0, LANES - 2 * H_B))).astype(BF16)
        gate_bias = jnp.pad(jnp.concatenate([mlstm_b_i[i], mlstm_b_f[i]]),
                            (0, LANES - 2 * H_B)).reshape(1, LANES)

        tm_in = _pick(t, 1024)
        tn_in = _pick(w_main.shape[1], 1408)
        z, gc = _inproj(x2d, norm_mix[i].reshape(1, d), w_main, w_gate, tm_in, tn_in)
        z3 = z.reshape(b, s, z.shape[1])
        gc3 = gc.reshape(b, s, LANES)

        ts = _pick(s, 1024)
        ya = _hgrn(z3, hgrn_lb[i:i + 2], hgrn_norm[i].reshape(1, wa_cols), tril, ts)
        yb = _mlstm(z3, gc3, mlstm_conv_w[i], mlstm_conv_b[i].reshape(1, -1), gate_bias,
                    mlstm_norm[i].reshape(1, wb_cols), tril, ts)

        tm_mg = _pick(t, 256)
        upper = jnp.triu(jnp.ones((tm_mg, tm_mg), F32), 1).astype(BF16)
        x1, hp, idx, gate_rows, gcol, rank, cnt = _merge(
            ya.reshape(t, wa_cols), yb.reshape(t, wb_cols), z, x2d,
            w_branch_a[i].astype(BF16), w_branch_b[i].astype(BF16), w_out[i].astype(BF16),
            norm_moe[i].reshape(1, d), router_w[i].T, router_b[i].reshape(N_EXPERTS, 1),
            upper, tm_mg)
        del gate_rows

        counts = cnt[:, 0].astype(I32)
        padded = (counts + EXPERT_TILE - 1) // EXPERT_TILE * EXPERT_TILE
        pad_end = jnp.cumsum(padded)
        pad_start = pad_end - padded
        pos = pad_start[idx] + rank
        n_rows = t * TOP_K + N_EXPERTS * EXPERT_TILE
        n_tiles = n_rows // EXPERT_TILE
        tile_row = jnp.arange(n_tiles, dtype=I32) * EXPERT_TILE
        tile_valid = (tile_row < pad_end[-1]).astype(I32)
        tile_expert = jnp.minimum(jnp.searchsorted(pad_end, tile_row, side='right'),
                                  N_EXPERTS - 1).astype(I32)
        last_e = tile_expert[jnp.maximum(pad_end[-1] // EXPERT_TILE - 1, 0)]
        tile_expert = jnp.where(tile_valid > 0, tile_expert, last_e)

        xs = _dispatch(pos, hp, jnp.zeros((n_rows, d), F32), _pick(t, 2048))
        ys = _experts(tile_expert, tile_valid, xs,
                      exp_w_gu[i].astype(BF16), exp_b_gu[i].reshape(N_EXPERTS, 1, -1),
                      exp_w_dn[i].astype(BF16), exp_b_dn[i].reshape(N_EXPERTS, 1, d))

        is_last = i == depth - 1
        assert is_last, "the fused final-norm epilogue assumes a single layer"
        x2d = _combine(pos, ys, x1, gcol, p[i].reshape(t, -1), norm_ple[i].reshape(1, d),
                       w_ple_gate[i].astype(BF16), w_ple[i].astype(BF16),
                       norm_final.reshape(1, d), _pick(t, 256))
    return x2d.reshape(b, s, d)
```

```python
import functools

import jax
import jax.numpy as jnp
from jax import lax
from jax.experimental import pallas as pl
from jax.experimental.pallas import tpu as pltpu

F32 = jnp.float32
BF16 = jnp.bfloat16
U32 = jnp.uint32
I32 = jnp.int32

EPS = 1e-6
CHUNK = 64
H_A, DK_A, DV_A = 8, 128, 128
H_B, DQK_B, DV_B = 4, 128, 256
CONV_W = 4
M_INIT = -1e30
N_EXPERTS = 32
TOP_K = 4
SWIGLU_LIMIT = 7.0
SWIGLU_ALPHA = 1.702

V7X_VMEM_LIMIT_BYTES = 56 * 1024 * 1024
LANES = 128

EXPERT_TILE = 512
FF_CHUNK = 512


def _cparams(sem):
    return pltpu.CompilerParams(dimension_semantics=sem, vmem_limit_bytes=V7X_VMEM_LIMIT_BYTES)


def _sigmoid(x):
    return jax.nn.sigmoid(x)


def _split3(x):
    hi = x.astype(BF16)
    r1 = x - hi.astype(F32)
    mid = r1.astype(BF16)
    lo = (r1 - mid.astype(F32)).astype(BF16)
    return hi, mid, lo


def _cumsum_rows(tril, x):
    hi, mid, lo = _split3(x)
    acc = jnp.dot(tril, hi, preferred_element_type=F32)
    acc = acc + jnp.dot(tril, mid, preferred_element_type=F32)
    return acc + jnp.dot(tril, lo, preferred_element_type=F32)


def _dot_tn(a, b):
    return lax.dot_general(a, b, (((0,), (0,)), ((), ())), preferred_element_type=F32)


def _dot_nt(a, b):
    return lax.dot_general(a, b, (((1,), (1,)), ((), ())), preferred_element_type=F32)


def _inproj_kernel(x_ref, g_ref, w_ref, wg_ref, z_ref, gc_ref, h_scr):
    @pl.when(pl.program_id(1) == 0)
    def _():
        x = x_ref[...]
        ms = jnp.mean(x * x, axis=-1, keepdims=True)
        hb = (x * lax.rsqrt(ms + EPS) * g_ref[...]).astype(BF16)
        h_scr[...] = hb
        gc_ref[...] = jnp.dot(hb, wg_ref[...], preferred_element_type=F32)

    z_ref[...] = jnp.dot(h_scr[...], w_ref[...], preferred_element_type=F32).astype(BF16)


def _inproj(x2d, gain, w_main, w_gate, tm, tn):
    t, d = x2d.shape
    n = w_main.shape[1]
    return pl.pallas_call(
        _inproj_kernel,
        out_shape=(jax.ShapeDtypeStruct((t, n), BF16), jax.ShapeDtypeStruct((t, LANES), F32)),
        grid=(t // tm, n // tn),
        in_specs=[
            pl.BlockSpec((tm, d), lambda i, j: (i, 0)),
            pl.BlockSpec((1, d), lambda i, j: (0, 0)),
            pl.BlockSpec((d, tn), lambda i, j: (0, j)),
            pl.BlockSpec((d, LANES), lambda i, j: (0, 0)),
        ],
        out_specs=(
            pl.BlockSpec((tm, tn), lambda i, j: (i, j)),
            pl.BlockSpec((tm, LANES), lambda i, j: (i, 0)),
        ),
        scratch_shapes=[pltpu.VMEM((tm, d), BF16)],
        compiler_params=_cparams(("parallel", "arbitrary")),
        name="inproj",
    )(x2d, gain, w_main, w_gate)


def _hgrn_kernel(q_ref, f_ref, v_ref, g_ref, lb_ref, gain_ref, tril_ref, o_ref,
                 st_ref, qd_s, kd_s, ke_s, qe_s):
    @pl.when(pl.program_id(1) == 0)
    def _():
        st_ref[...] = jnp.zeros_like(st_ref)

    lbr = lb_ref[...]
    mx = jnp.maximum(lbr[0:1], lbr[1:2])
    e0 = jnp.exp(lbr[0:1] - mx)
    e1 = jnp.exp(lbr[1:2] - mx)
    lb = e0 / (e0 + e1)
    one_m_lb = 1.0 - lb
    tril = tril_ref[...]
    gain = gain_ref[...]
    rows = lax.broadcasted_iota(I32, (CHUNK, CHUNK), 0)
    cols = lax.broadcasted_iota(I32, (CHUNK, CHUNK), 1)
    causal = rows >= cols
    n_chunks = q_ref.shape[0] // CHUNK

    def chunk(c, carry):
        r0 = pl.multiple_of(c * CHUNK, CHUNK)
        q = q_ref[pl.ds(r0, CHUNK), :].astype(F32)
        fl = f_ref[pl.ds(r0, CHUNK), :].astype(F32)
        log_f = jnp.log(lb + one_m_lb * _sigmoid(fl))
        k = one_m_lb * _sigmoid(-fl)
        qs = q * _sigmoid(q)
        bc = _cumsum_rows(tril, log_f)
        b_ref = bc[CHUNK // 2:CHUNK // 2 + 1, :]
        b_end = bc[CHUNK - 1:CHUNK, :]
        qd_s[...] = (qs * jnp.exp(bc - b_ref)).astype(BF16)
        kd_s[...] = (k * jnp.exp(b_ref - bc)).astype(BF16)
        ke_s[...] = (k * jnp.exp(b_end - bc)).astype(BF16)
        qe_s[...] = (qs * jnp.exp(bc)).astype(BF16)
        decay = jnp.exp(b_end)
        for h in range(H_A):
            sl = slice(h * DK_A, (h + 1) * DK_A)
            scores = _dot_nt(qd_s[:, sl], kd_s[:, sl])
            scores = jnp.where(causal, scores, 0.0).astype(BF16)
            vh = v_ref[pl.ds(r0, CHUNK), sl]
            st = st_ref[h]
            o = jnp.dot(scores, vh, preferred_element_type=F32)
            o = o + _dot_nt(qe_s[:, sl], st.astype(BF16))
            u_t = _dot_tn(vh, ke_s[:, sl])
            st_ref[h] = st * decay[:, sl] + u_t
            ms = jnp.mean(o * o, axis=-1, keepdims=True)
            gt = g_ref[pl.ds(r0, CHUNK), sl].astype(F32)
            y = o * lax.rsqrt(ms + EPS) * gain[:, sl] * (gt * _sigmoid(gt))
            o_ref[pl.ds(r0, CHUNK), sl] = y.astype(BF16)
        return carry

    lax.fori_loop(0, n_chunks, chunk, 0)


def _hgrn(z3, lb_raw, gain, tril, ts):
    b, s, _ = z3.shape
    w = H_A * DK_A

    def zspec(col):
        return pl.BlockSpec((None, ts, w), lambda i, j, col=col: (i, j, col))

    return pl.pallas_call(
        _hgrn_kernel,
        out_shape=jax.ShapeDtypeStruct((b, s, w), BF16),
        grid=(b, s // ts),
        in_specs=[zspec(0), zspec(1), zspec(2), zspec(3),
                  pl.BlockSpec((2, w), lambda i, j: (0, 0)),
                  pl.BlockSpec((1, w), lambda i, j: (0, 0)),
                  pl.BlockSpec((CHUNK, CHUNK), lambda i, j: (0, 0))],
        out_specs=pl.BlockSpec((None, ts, w), lambda i, j: (i, j, 0)),
        scratch_shapes=[pltpu.VMEM((H_A, DV_A, DK_A), F32),
                        pltpu.VMEM((CHUNK, w), BF16), pltpu.VMEM((CHUNK, w), BF16),
                        pltpu.VMEM((CHUNK, w), BF16), pltpu.VMEM((CHUNK, w), BF16)],
        compiler_params=_cparams(("parallel", "arbitrary")),
        name="hgrn2",
    )(z3, z3, z3, z3, lb_raw, gain, tril)


def _log_sigmoid(x):
    return jnp.minimum(x, 0.0) - jnp.log1p(jnp.exp(-jnp.abs(x)))


def _mlstm_kernel(qk_ref, v_ref, o_in_ref, gc_ref, cw_ref, cb_ref, gb_ref, gain_ref, tril_ref,
                  o_ref, c_ref, n_ref, m_ref, tail_ref, ext_s):
    @pl.when(pl.program_id(1) == 0)
    def _():
        c_ref[...] = jnp.zeros_like(c_ref)
        n_ref[...] = jnp.zeros_like(n_ref)
        m_ref[...] = jnp.full_like(m_ref, M_INIT)
        tail_ref[...] = jnp.zeros_like(tail_ref)

    tril = tril_ref[...]
    gain = gain_ref[...]
    cw = cw_ref[...]
    cb = cb_ref[...]
    gb = gb_ref[...]
    rows = lax.broadcasted_iota(I32, (CHUNK, CHUNK), 0)
    cols = lax.broadcasted_iota(I32, (CHUNK, CHUNK), 1)
    causal = rows >= cols
    n_chunks = qk_ref.shape[0] // CHUNK
    hq = H_B * DQK_B
    scale = DQK_B ** -0.5

    def chunk(c, carry):
        r0 = pl.multiple_of(c * CHUNK, CHUNK)
        u = qk_ref[pl.ds(r0, CHUNK), :].astype(F32)
        ext_s[0:8, :] = tail_ref[...]
        ext_s[8:8 + CHUNK, :] = u
        tail_ref[...] = u[CHUNK - 8:CHUNK, :]
        conv = cb + cw[CONV_W - 1:CONV_W, :] * u
        for j in range(CONV_W - 1):
            lag = CONV_W - 1 - j
            conv = conv + cw[j:j + 1, :] * ext_s[8 - lag:8 - lag + CHUNK, :]
        qk = conv * _sigmoid(conv)
        q = qk[:, :hq]
        k = qk[:, hq:] * scale

        g = gc_ref[pl.ds(r0, CHUNK), :] + gb
        fc = _cumsum_rows(tril, _log_sigmoid(g))
        zpad = jnp.zeros((LANES - CHUNK, LANES), F32)
        g_t = jnp.concatenate([g, zpad], axis=0).T
        fc_t = jnp.concatenate([fc, zpad], axis=0).T

        for h in range(H_B):
            ig_c = g[:, h:h + 1]
            fc_c = fc[:, H_B + h:H_B + h + 1]
            ig_r = g_t[h:h + 1, 0:CHUNK]
            fc_r = fc_t[H_B + h:H_B + h + 1, 0:CHUNK]
            g_tot = fc_c[CHUNK - 1:CHUNK, :]
            qh = q[:, h * DQK_B:(h + 1) * DQK_B]
            kh = k[:, h * DQK_B:(h + 1) * DQK_B]
            vh = v_ref[pl.ds(r0, CHUNK), h * DV_B:(h + 1) * DV_B]
            qb = qh.astype(BF16)
            c_st = c_ref[h]
            n_st = n_ref[h]
            m_st = m_ref[h][0:1, 0:1]

            d_log = jnp.where(causal, fc_c - fc_r + ig_r, -jnp.inf)
            a_int = fc_c + m_st
            m_t = jnp.maximum(jnp.max(d_log, axis=-1, keepdims=True), a_int)
            w_in = jnp.exp(d_log - m_t)
            w_state = jnp.exp(a_int - m_t)
            qkw = _dot_nt(qb, kh.astype(BF16)) * w_in
            num = (jnp.dot(qkw.astype(BF16), vh, preferred_element_type=F32)
                   + w_state * jnp.dot(qb, c_st.astype(BF16), preferred_element_type=F32))
            den = (jnp.sum(qkw, axis=-1, keepdims=True)
                   + w_state * jnp.sum(qh * n_st, axis=-1, keepdims=True))
            hh = num / jnp.maximum(jnp.abs(den), jnp.exp(-m_t))

            a_end = g_tot - fc_c + ig_c
            m_loc = jnp.max(a_end, axis=0, keepdims=True)
            kw = kh * jnp.exp(a_end - m_loc)
            c_loc = _dot_tn(kw.astype(BF16), vh)
            n_loc = jnp.sum(kw, axis=0, keepdims=True)
            m_new = jnp.maximum(g_tot + m_st, m_loc)
            s_prev = jnp.exp(g_tot + m_st - m_new)
            s_loc = jnp.exp(m_loc - m_new)
            c_ref[h] = s_prev * c_st + s_loc * c_loc
            n_ref[h] = s_prev * n_st + s_loc * n_loc
            m_ref[h] = jnp.broadcast_to(m_new, m_ref.shape[1:])

            vs = slice(h * DV_B, (h + 1) * DV_B)
            ms = jnp.mean(hh * hh, axis=-1, keepdims=True)
            og = o_in_ref[pl.ds(r0, CHUNK), vs].astype(F32)
            y = hh * lax.rsqrt(ms + EPS) * gain[:, vs] * _sigmoid(og)
            o_ref[pl.ds(r0, CHUNK), vs] = y.astype(BF16)
        return carry

    lax.fori_loop(0, n_chunks, chunk, 0)


def _mlstm(z3, gc3, conv_w, conv_b, gate_bias, gain, tril, ts):
    b, s, _ = z3.shape
    w = H_B * DV_B

    def zspec(col):
        return pl.BlockSpec((None, ts, w), lambda i, j, col=col: (i, j, col))

    def const(shape):
        return pl.BlockSpec(shape, lambda i, j: (0,) * len(shape))

    return pl.pallas_call(
        _mlstm_kernel,
        out_shape=jax.ShapeDtypeStruct((b, s, w), BF16),
        grid=(b, s // ts),
        in_specs=[zspec(4), zspec(5), zspec(6),
                  pl.BlockSpec((None, ts, LANES), lambda i, j: (i, j, 0)),
                  const((CONV_W, w)), const((1, w)), const((1, LANES)), const((1, w)),
                  const((CHUNK, CHUNK))],
        out_specs=pl.BlockSpec((None, ts, w), lambda i, j: (i, j, 0)),
        scratch_shapes=[pltpu.VMEM((H_B, DQK_B, DV_B), F32),
                        pltpu.VMEM((H_B, 1, DQK_B), F32),
                        pltpu.VMEM((H_B, 8, LANES), F32),
                        pltpu.VMEM((8, w), F32),
                        pltpu.VMEM((8 + CHUNK, w), F32)],
        compiler_params=_cparams(("parallel", "arbitrary")),
        name="mlstm",
    )(z3, z3, z3, gc3, conv_w, conv_b, gate_bias, gain, tril)


def _merge_kernel(ya_ref, yb_ref, ra0_ref, ra1_ref, rb0_ref, rb1_ref, x_ref,
                  wa_ref, wb_ref, wo_ref, g_ref, rw_ref, rb_ref, upper_ref,
                  x1_ref, hp_ref, idx_ref, gate_ref, gcol_ref, rank_ref, cnt_ref, carry_ref):
    @pl.when(pl.program_id(0) == 0)
    def _():
        carry_ref[...] = jnp.zeros_like(carry_ref)

    half = wo_ref.shape[0] // 2
    pa = jnp.dot(ya_ref[...], wa_ref[...], preferred_element_type=F32)
    pb = jnp.dot(yb_ref[...], wb_ref[...], preferred_element_type=F32)
    mix0 = (_sigmoid(ra0_ref[...].astype(F32)) * pa[:, :half]
            + _sigmoid(rb0_ref[...].astype(F32)) * pb[:, :half]).astype(BF16)
    mix1 = (_sigmoid(ra1_ref[...].astype(F32)) * pa[:, half:]
            + _sigmoid(rb1_ref[...].astype(F32)) * pb[:, half:]).astype(BF16)
    x1 = (x_ref[...] + jnp.dot(mix0, wo_ref[0:half, :], preferred_element_type=F32)
          + jnp.dot(mix1, wo_ref[half:, :], preferred_element_type=F32))
    x1_ref[...] = x1

    ms = jnp.mean(x1 * x1, axis=-1, keepdims=True)
    h2 = x1 * lax.rsqrt(ms + EPS) * g_ref[...]
    hb = h2.astype(BF16)
    hp_ref[...] = h2

    h_lo = (h2 - hb.astype(F32)).astype(BF16)
    rw = rw_ref[...]
    rw_hi = rw.astype(BF16)
    rw_lo = (rw - rw_hi.astype(F32)).astype(BF16)
    logits = (_dot_nt(rw_hi, hb) + _dot_nt(rw_hi, h_lo) + _dot_nt(rw_lo, hb)) + rb_ref[...]

    tm = logits.shape[1]
    erow = lax.broadcasted_iota(I32, (N_EXPERTS, tm), 0)
    work = logits
    vals, idxs = [], []
    for _ in range(TOP_K):
        mval = jnp.max(work, axis=0, keepdims=True)
        idx = jnp.min(jnp.where(work == mval, erow, N_EXPERTS), axis=0, keepdims=True)
        work = jnp.where(erow == idx, -jnp.inf, work)
        vals.append(mval)
        idxs.append(idx)
    exps = [jnp.exp(v - vals[0]) for v in vals]
    denom = exps[0] + exps[1] + exps[2] + exps[3]
    gates = [e / denom for e in exps]

    onehots = [(erow == idx).astype(F32) for idx in idxs]
    mask = onehots[0] + onehots[1] + onehots[2] + onehots[3]
    before = jnp.dot(mask.astype(BF16), upper_ref[...], preferred_element_type=F32)
    before = before + carry_ref[:, 0:1]
    carry = carry_ref[...] + jnp.sum(mask, axis=1, keepdims=True)
    carry_ref[...] = carry
    cnt_ref[...] = carry

    idx_ref[...] = jnp.concatenate(idxs, axis=0)
    gate_rows = jnp.concatenate(gates, axis=0)
    gate_ref[...] = gate_rows
    rank_ref[...] = jnp.concatenate(
        [jnp.sum(oh * before, axis=0, keepdims=True) for oh in onehots], axis=0).astype(I32)
    padded = jnp.concatenate([gate_rows, jnp.zeros((LANES - TOP_K, tm), F32)], axis=0)
    gcol_ref[...] = padded.T


def _merge(ya, yb, z, x2d, wa, wb, wo, gain, rw_t, rb_col, upper, tm):
    t, d = x2d.shape
    half = d // 2
    ra_blk = (4 * H_A * DK_A + 3 * H_B * DV_B) // half

    def rows(width, col=0):
        return pl.BlockSpec((tm, width), lambda i, col=col: (i, col))

    def const(shape):
        return pl.BlockSpec(shape, lambda i: (0,) * len(shape), pipeline_mode=pl.Buffered(1))

    out_shape = (
        jax.ShapeDtypeStruct((t, d), F32),
        jax.ShapeDtypeStruct((t, d), F32),
        jax.ShapeDtypeStruct((TOP_K, t), I32),
        jax.ShapeDtypeStruct((TOP_K, t), F32),
        jax.ShapeDtypeStruct((t, LANES), F32),
        jax.ShapeDtypeStruct((TOP_K, t), I32),
        jax.ShapeDtypeStruct((N_EXPERTS, LANES), F32),
    )
    return pl.pallas_call(
        _merge_kernel,
        out_shape=out_shape,
        grid=(t // tm,),
        in_specs=[rows(ya.shape[1]), rows(yb.shape[1]),
                  rows(half, ra_blk), rows(half, ra_blk + 1), rows(half, ra_blk + 2),
                  rows(half, ra_blk + 3), rows(d),
                  const(wa.shape), const(wb.shape), const(wo.shape), const((1, d)),
                  const(rw_t.shape), const((N_EXPERTS, 1)), const((tm, tm))],
        out_specs=(rows(d), rows(d),
                   pl.BlockSpec((TOP_K, tm), lambda i: (0, i)),
                   pl.BlockSpec((TOP_K, tm), lambda i: (0, i)),
                   rows(LANES),
                   pl.BlockSpec((TOP_K, tm), lambda i: (0, i)),
                   pl.BlockSpec((N_EXPERTS, LANES), lambda i: (0, 0))),
        scratch_shapes=[pltpu.VMEM((N_EXPERTS, LANES), F32)],
        compiler_params=_cparams(("arbitrary",)),
        name="merge_router",
    )(ya, yb, z, z, z, z, x2d, wa, wb, wo, gain, rw_t, rb_col, upper)


def _dispatch_kernel(pad_end_ref, padded_ref, pos_ref, src_ref, dst_ref, zero_s, zsem, sem):
    td = src_ref.shape[0]

    @pl.when(pl.program_id(0) == 0)
    def _():
        zero_s[...] = jnp.zeros_like(zero_s)

        def tail_copy(e):
            start = pl.multiple_of(pad_end_ref[e] - EXPERT_TILE, EXPERT_TILE)
            return pltpu.make_async_copy(zero_s, dst_ref.at[pl.ds(start, EXPERT_TILE)], zsem)

        for e in range(N_EXPERTS):
            @pl.when(padded_ref[e] > 0)
            def _():
                tail_copy(e).start()
        for e in range(N_EXPERTS):
            @pl.when(padded_ref[e] > 0)
            def _():
                tail_copy(e).wait()

    def issue(r, carry):
        for k in range(TOP_K):
            pltpu.make_async_copy(src_ref.at[pl.ds(r, 1)],
                                  dst_ref.at[pl.ds(pos_ref[k, r], 1)], sem).start()
        return carry

    lax.fori_loop(0, td, issue, 0, unroll=8)
    for k in range(TOP_K):
        pltpu.make_async_copy(src_ref, dst_ref.at[pl.ds(0, td)], sem).wait()


def _dispatch(pad_end, padded, pos, src, n_rows, td):
    t, d = src.shape
    grid_spec = pltpu.PrefetchScalarGridSpec(
        num_scalar_prefetch=2,
        grid=(t // td,),
        in_specs=[pl.BlockSpec((TOP_K, td), lambda i, pe, pd: (0, i), memory_space=pltpu.SMEM),
                  pl.BlockSpec((td, d), lambda i, pe, pd: (i, 0))],
        out_specs=pl.BlockSpec(memory_space=pl.ANY),
        scratch_shapes=[pltpu.VMEM((EXPERT_TILE, d), src.dtype),
                        pltpu.SemaphoreType.DMA, pltpu.SemaphoreType.DMA],
    )
    return pl.pallas_call(
        _dispatch_kernel,
        out_shape=jax.ShapeDtypeStruct((n_rows, d), src.dtype),
        grid_spec=grid_spec,
        compiler_params=pltpu.CompilerParams(dimension_semantics=("arbitrary",),
                                             vmem_limit_bytes=V7X_VMEM_LIMIT_BYTES,
                                             has_side_effects=True),
        name="dispatch",
    )(pad_end, padded, pos, src)


def _expert_kernel(te_ref, tv_ref, xs_ref, wa_ref, wu_ref, ba_ref, bu_ref, wd_ref, bd_ref,
                   o_ref, xb_s):
    i = pl.program_id(0)
    c = pl.program_id(1)
    valid = tv_ref[i] > 0

    @pl.when(jnp.logical_and(valid, c == 0))
    def _():
        xb_s[...] = xs_ref[...].astype(BF16)

    @pl.when(valid)
    def _():
        xb = xb_s[...]
        a = jnp.dot(xb, wa_ref[...], preferred_element_type=F32) + ba_ref[...]
        u = jnp.dot(xb, wu_ref[...], preferred_element_type=F32) + bu_ref[...]
        a = jnp.minimum(a, SWIGLU_LIMIT)
        u = jnp.clip(u, -SWIGLU_LIMIT, SWIGLU_LIMIT)
        act = ((u + 1.0) * a * _sigmoid(SWIGLU_ALPHA * a)).astype(BF16)
        y = jnp.dot(act, wd_ref[...], preferred_element_type=F32)

        @pl.when(c == 0)
        def _():
            o_ref[...] = y + bd_ref[...]

        @pl.when(c != 0)
        def _():
            o_ref[...] += y

    @pl.when(jnp.logical_and(jnp.logical_not(valid), c == 0))
    def _():
        o_ref[...] = jnp.zeros_like(o_ref)


def _experts(tile_expert, tile_valid, xs, w_gu, b_gu, w_dn, b_dn):
    n_rows, d = xs.shape
    d_ff = w_dn.shape[1]
    n_tiles = n_rows // EXPERT_TILE
    n_c = d_ff // FF_CHUNK
    last = n_c - 1

    def cc(c, tv, i):
        return jnp.where(tv[i] > 0, c, last)

    grid_spec = pltpu.PrefetchScalarGridSpec(
        num_scalar_prefetch=2,
        grid=(n_tiles, n_c),
        in_specs=[
            pl.BlockSpec((EXPERT_TILE, d), lambda i, c, te, tv: (jnp.minimum(i, tv[n_tiles]), 0)),
            pl.BlockSpec((None, d, FF_CHUNK), lambda i, c, te, tv: (te[i], 0, cc(c, tv, i))),
            pl.BlockSpec((None, d, FF_CHUNK), lambda i, c, te, tv: (te[i], 0, n_c + cc(c, tv, i))),
            pl.BlockSpec((None, 1, FF_CHUNK), lambda i, c, te, tv: (te[i], 0, cc(c, tv, i))),
            pl.BlockSpec((None, 1, FF_CHUNK), lambda i, c, te, tv: (te[i], 0, n_c + cc(c, tv, i))),
            pl.BlockSpec((None, FF_CHUNK, d), lambda i, c, te, tv: (te[i], cc(c, tv, i), 0)),
            pl.BlockSpec((None, 1, d), lambda i, c, te, tv: (te[i], 0, 0)),
        ],
        out_specs=pl.BlockSpec((EXPERT_TILE, d), lambda i, c, te, tv: (i, 0)),
        scratch_shapes=[pltpu.VMEM((EXPERT_TILE, d), BF16)],
    )
    return pl.pallas_call(
        _expert_kernel,
        out_shape=jax.ShapeDtypeStruct((n_rows, d), F32),
        grid_spec=grid_spec,
        compiler_params=_cparams(("arbitrary", "arbitrary")),
        name="experts",
    )(tile_expert, tile_valid, xs, w_gu, w_gu, b_gu, b_gu, w_dn, b_dn)


def _combine_kernel(pos_ref, posn_ref, ys_ref, x1_ref, gcol_ref, p_ref, g_ple_ref, wpg_ref,
                    wple_ref, g_fin_ref, o_ref, buf, sems):
    tc = x1_ref.shape[0]
    i = pl.program_id(0)
    slot = lax.rem(i, 2)

    def gather(idx_ref, dst_slot):
        def issue(r, carry):
            for k in range(TOP_K):
                pltpu.make_async_copy(ys_ref.at[pl.ds(idx_ref[k, r], 1)],
                                      buf.at[dst_slot, k, pl.ds(r, 1)], sems.at[dst_slot]).start()
            return carry
        lax.fori_loop(0, tc, issue, 0, unroll=8)

    @pl.when(i == 0)
    def _():
        gather(pos_ref, slot)

    @pl.when(i + 1 < pl.num_programs(0))
    def _():
        gather(posn_ref, 1 - slot)

    ple = jnp.dot(p_ref[...].astype(BF16), wple_ref[...], preferred_element_type=F32)
    for k in range(TOP_K):
        pltpu.make_async_copy(ys_ref.at[pl.ds(0, tc)], buf.at[slot, k], sems.at[slot]).wait()

    gcol = gcol_ref[...]
    moe = buf[slot, 0] * gcol[:, 0:1]
    for k in range(1, TOP_K):
        moe = moe + buf[slot, k] * gcol[:, k:k + 1]
    x2 = x1_ref[...] + moe
    ms = jnp.mean(x2 * x2, axis=-1, keepdims=True)
    h3 = (x2 * lax.rsqrt(ms + EPS) * g_ple_ref[...]).astype(BF16)
    gate = _sigmoid(jnp.dot(h3, wpg_ref[...], preferred_element_type=F32))
    x3 = x2 + ple * gate
    ms3 = jnp.mean(x3 * x3, axis=-1, keepdims=True)
    o_ref[...] = x3 * lax.rsqrt(ms3 + EPS) * g_fin_ref[...]


def _combine(pos, ys, x1, gcol, p2d, g_ple, wpg, wple, g_fin, tc):
    t, d = x1.shape
    n_steps = t // tc

    def rows(width):
        return pl.BlockSpec((tc, width), lambda i: (i, 0))

    def const(shape):
        return pl.BlockSpec(shape, lambda i: (0,) * len(shape), pipeline_mode=pl.Buffered(1))

    return pl.pallas_call(
        _combine_kernel,
        out_shape=jax.ShapeDtypeStruct((t, d), F32),
        grid=(t // tc,),
        in_specs=[pl.BlockSpec((TOP_K, tc), lambda i: (0, i), memory_space=pltpu.SMEM),
                  pl.BlockSpec((TOP_K, tc), lambda i: (0, jnp.minimum(i + 1, n_steps - 1)),
                               memory_space=pltpu.SMEM),
                  pl.BlockSpec(memory_space=pl.ANY),
                  rows(d), rows(LANES), rows(p2d.shape[1]),
                  const((1, d)), const(wpg.shape), const(wple.shape), const((1, d))],
        out_specs=rows(d),
        scratch_shapes=[pltpu.VMEM((2, TOP_K, tc, d), F32), pltpu.SemaphoreType.DMA((2,))],
        compiler_params=_cparams(("arbitrary",)),
        name="combine_ple",
    )(pos, pos, ys, x1, gcol, p2d, g_ple, wpg, wple, g_fin)


def _pick(n, pref):
    return pref if n % pref == 0 else n


def kernel(x, p, norm_mix, w_in, hgrn_lb, hgrn_norm, mlstm_conv_w, mlstm_conv_b, mlstm_b_i,
           mlstm_b_f, mlstm_norm, w_branch_a, w_branch_b, w_out, norm_moe, router_w, router_b,
           exp_w_gu, exp_b_gu, exp_w_dn, exp_b_dn, norm_ple, w_ple, w_ple_gate, norm_final):
    b, s, d = x.shape
    t = b * s
    depth = norm_mix.shape[0]
    wa_cols = H_A * DK_A
    wb_cols = H_B * DV_B
    g0 = 4 * wa_cols + 2 * wb_cols
    g1 = g0 + 2 * H_B
    tril = jnp.tril(jnp.ones((CHUNK, CHUNK), F32)).astype(BF16)
    x2d = x.reshape(t, d)

    for i in range(depth):
        w_i = w_in[i]
        w_main = jnp.concatenate([w_i[:, :g0], w_i[:, g1:]], axis=1).astype(BF16)
        w_gate = jnp.pad(w_i[:, g0:g1], ((0, 0), (0, LANES - 2 * H_B))).astype(BF16)
        gate_bias = jnp.pad(jnp.concatenate([mlstm_b_i[i], mlstm_b_f[i]]),
                            (0, LANES - 2 * H_B)).reshape(1, LANES)

        tm_in = _pick(t, 1024)
        tn_in = _pick(w_main.shape[1], 1408)
        z, gc = _inproj(x2d, norm_mix[i].reshape(1, d), w_main, w_gate, tm_in, tn_in)
        z3 = z.reshape(b, s, z.shape[1])
        gc3 = gc.reshape(b, s, LANES)

        ts = _pick(s, 1024)
        ya = _hgrn(z3, hgrn_lb[i:i + 2], hgrn_norm[i].reshape(1, wa_cols), tril, ts)
        yb = _mlstm(z3, gc3, mlstm_conv_w[i], mlstm_conv_b[i].reshape(1, -1), gate_bias,
                    mlstm_norm[i].reshape(1, wb_cols), tril, ts)

        tm_mg = _pick(t, 256)
        upper = jnp.triu(jnp.ones((tm_mg, tm_mg), F32), 1).astype(BF16)
        x1, hp, idx, gate_rows, gcol, rank, cnt = _merge(
            ya.reshape(t, wa_cols), yb.reshape(t, wb_cols), z, x2d,
            w_branch_a[i].astype(BF16), w_branch_b[i].astype(BF16), w_out[i].astype(BF16),
            norm_moe[i].reshape(1, d), router_w[i].T, router_b[i].reshape(N_EXPERTS, 1),
            upper, tm_mg)
        del gate_rows

        counts = cnt[:, 0].astype(I32)
        padded = (counts + EXPERT_TILE - 1) // EXPERT_TILE * EXPERT_TILE
        pad_end = jnp.cumsum(padded)
        pad_start = pad_end - padded
        e_ids = jnp.arange(N_EXPERTS, dtype=I32)
        pos = rank + jnp.sum(jnp.where(idx[..., None] == e_ids, pad_start, 0), axis=-1)
        n_rows = t * TOP_K + N_EXPERTS * EXPERT_TILE
        n_tiles = n_rows // EXPERT_TILE
        tile_row = jnp.arange(n_tiles, dtype=I32) * EXPERT_TILE
        n_valid = pad_end[-1] // EXPERT_TILE
        last_valid = jnp.maximum(n_valid - 1, 0)
        tile_valid = (tile_row < pad_end[-1]).astype(I32)
        tile_expert = jnp.sum((tile_row[:, None] >= pad_end[None, :]).astype(I32), axis=1)
        tile_expert = jnp.minimum(tile_expert, N_EXPERTS - 1)
        last_e = jnp.sum(jnp.where(jnp.arange(n_tiles) == last_valid, tile_expert, 0))
        tile_expert = jnp.where(tile_valid > 0, tile_expert, last_e).astype(I32)
        tile_valid = jnp.concatenate([tile_valid, last_valid.reshape(1).astype(I32)])

        xs = _dispatch(pad_end.astype(I32), padded, pos, hp, n_rows, _pick(t, 512))
        ys = _experts(tile_expert, tile_valid, xs,
                      exp_w_gu[i].astype(BF16), exp_b_gu[i].reshape(N_EXPERTS, 1, -1),
                      exp_w_dn[i].astype(BF16), exp_b_dn[i].reshape(N_EXPERTS, 1, d))

        is_last = i == depth - 1
        assert is_last, "the fused final-norm epilogue assumes a single layer"
        x2d = _combine(pos, ys, x1, gcol, p[i].reshape(t, -1), norm_ple[i].reshape(1, d),
                       w_ple_gate[i].astype(BF16), w_ple[i].astype(BF16),
                       norm_final.reshape(1, d), _pick(t, 256))
    return x2d.reshape(b, s, d)
```

```python
import functools

import jax
import jax.numpy as jnp
from jax import lax
from jax.experimental import pallas as pl
from jax.experimental.pallas import tpu as pltpu

F32 = jnp.float32
BF16 = jnp.bfloat16
U32 = jnp.uint32
I32 = jnp.int32

EPS = 1e-6
CHUNK = 64
H_A, DK_A, DV_A = 8, 128, 128
H_B, DQK_B, DV_B = 4, 128, 256
CONV_W = 4
M_INIT = -1e30
N_EXPERTS = 32
TOP_K = 4
SWIGLU_LIMIT = 7.0
SWIGLU_ALPHA = 1.702

V7X_VMEM_LIMIT_BYTES = 56 * 1024 * 1024
LANES = 128

EXPERT_TILE = 512
FF_CHUNK = 512
WEIGHT_COLS = 256


def _cparams(sem):
    return pltpu.CompilerParams(dimension_semantics=sem, vmem_limit_bytes=V7X_VMEM_LIMIT_BYTES)


def _sigmoid(x):
    return jax.nn.sigmoid(x)


def _split3(x):
    hi = x.astype(BF16)
    r1 = x - hi.astype(F32)
    mid = r1.astype(BF16)
    lo = (r1 - mid.astype(F32)).astype(BF16)
    return hi, mid, lo


def _cumsum_rows(tril, x):
    hi, mid, lo = _split3(x)
    acc = jnp.dot(tril, hi, preferred_element_type=F32)
    acc = acc + jnp.dot(tril, mid, preferred_element_type=F32)
    return acc + jnp.dot(tril, lo, preferred_element_type=F32)


def _dot_tn(a, b):
    return lax.dot_general(a, b, (((0,), (0,)), ((), ())), preferred_element_type=F32)


def _dot_nt(a, b):
    return lax.dot_general(a, b, (((1,), (1,)), ((), ())), preferred_element_type=F32)


def _inproj_kernel(x_ref, g_ref, w_ref, wg_ref, z_ref, gc_ref, h_scr):
    @pl.when(pl.program_id(1) == 0)
    def _():
        x = x_ref[...]
        ms = jnp.mean(x * x, axis=-1, keepdims=True)
        hb = (x * lax.rsqrt(ms + EPS) * g_ref[...]).astype(BF16)
        h_scr[...] = hb
        gc_ref[...] = jnp.dot(hb, wg_ref[...], preferred_element_type=F32)

    z_ref[...] = jnp.dot(h_scr[...], w_ref[...], preferred_element_type=F32).astype(BF16)


def _inproj(x2d, gain, w_main, w_gate, tm, tn):
    t, d = x2d.shape
    n = w_main.shape[1]
    return pl.pallas_call(
        _inproj_kernel,
        out_shape=(jax.ShapeDtypeStruct((t, n), BF16), jax.ShapeDtypeStruct((t, LANES), F32)),
        grid=(t // tm, n // tn),
        in_specs=[
            pl.BlockSpec((tm, d), lambda i, j: (i, 0)),
            pl.BlockSpec((1, d), lambda i, j: (0, 0)),
            pl.BlockSpec((d, tn), lambda i, j: (0, j)),
            pl.BlockSpec((d, LANES), lambda i, j: (0, 0)),
        ],
        out_specs=(
            pl.BlockSpec((tm, tn), lambda i, j: (i, j)),
            pl.BlockSpec((tm, LANES), lambda i, j: (i, 0)),
        ),
        scratch_shapes=[pltpu.VMEM((tm, d), BF16)],
        compiler_params=_cparams(("parallel", "arbitrary")),
        name="inproj",
    )(x2d, gain, w_main, w_gate)


def _hgrn_kernel(q_ref, f_ref, v_ref, g_ref, lb_ref, gain_ref, tril_ref, o_ref,
                 st_ref, qd_s, kd_s, ke_s, qe_s):
    @pl.when(pl.program_id(1) == 0)
    def _():
        st_ref[...] = jnp.zeros_like(st_ref)

    lbr = lb_ref[...]
    mx = jnp.maximum(lbr[0:1], lbr[1:2])
    e0 = jnp.exp(lbr[0:1] - mx)
    e1 = jnp.exp(lbr[1:2] - mx)
    lb = e0 / (e0 + e1)
    one_m_lb = 1.0 - lb
    tril = tril_ref[...]
    gain = gain_ref[...]
    rows = lax.broadcasted_iota(I32, (CHUNK, CHUNK), 0)
    cols = lax.broadcasted_iota(I32, (CHUNK, CHUNK), 1)
    causal = rows >= cols
    n_chunks = q_ref.shape[0] // CHUNK

    def chunk(c, carry):
        r0 = pl.multiple_of(c * CHUNK, CHUNK)
        q = q_ref[pl.ds(r0, CHUNK), :].astype(F32)
        fl = f_ref[pl.ds(r0, CHUNK), :].astype(F32)
        log_f = jnp.log(lb + one_m_lb * _sigmoid(fl))
        k = one_m_lb * _sigmoid(-fl)
        qs = q * _sigmoid(q)
        bc = _cumsum_rows(tril, log_f)
        b_ref = bc[CHUNK // 2:CHUNK // 2 + 1, :]
        b_end = bc[CHUNK - 1:CHUNK, :]
        qd_s[...] = (qs * jnp.exp(bc - b_ref)).astype(BF16)
        kd_s[...] = (k * jnp.exp(b_ref - bc)).astype(BF16)
        ke_s[...] = (k * jnp.exp(b_end - bc)).astype(BF16)
        qe_s[...] = (qs * jnp.exp(bc)).astype(BF16)
        decay = jnp.exp(b_end)
        for h in range(H_A):
            sl = slice(h * DK_A, (h + 1) * DK_A)
            scores = _dot_nt(qd_s[:, sl], kd_s[:, sl])
            scores = jnp.where(causal, scores, 0.0).astype(BF16)
            vh = v_ref[pl.ds(r0, CHUNK), sl]
            st = st_ref[h]
            o = jnp.dot(scores, vh, preferred_element_type=F32)
            o = o + _dot_nt(qe_s[:, sl], st.astype(BF16))
            u_t = _dot_tn(vh, ke_s[:, sl])
            st_ref[h] = st * decay[:, sl] + u_t
            ms = jnp.mean(o * o, axis=-1, keepdims=True)
            gt = g_ref[pl.ds(r0, CHUNK), sl].astype(F32)
            y = o * lax.rsqrt(ms + EPS) * gain[:, sl] * (gt * _sigmoid(gt))
            o_ref[pl.ds(r0, CHUNK), sl] = y.astype(BF16)
        return carry

    lax.fori_loop(0, n_chunks, chunk, 0)


def _hgrn(z3, lb_raw, gain, tril, ts):
    b, s, _ = z3.shape
    w = H_A * DK_A

    def zspec(col):
        return pl.BlockSpec((None, ts, w), lambda i, j, col=col: (i, j, col))

    return pl.pallas_call(
        _hgrn_kernel,
        out_shape=jax.ShapeDtypeStruct((b, s, w), BF16),
        grid=(b, s // ts),
        in_specs=[zspec(0), zspec(1), zspec(2), zspec(3),
                  pl.BlockSpec((2, w), lambda i, j: (0, 0)),
                  pl.BlockSpec((1, w), lambda i, j: (0, 0)),
                  pl.BlockSpec((CHUNK, CHUNK), lambda i, j: (0, 0))],
        out_specs=pl.BlockSpec((None, ts, w), lambda i, j: (i, j, 0)),
        scratch_shapes=[pltpu.VMEM((H_A, DV_A, DK_A), F32),
                        pltpu.VMEM((CHUNK, w), BF16), pltpu.VMEM((CHUNK, w), BF16),
                        pltpu.VMEM((CHUNK, w), BF16), pltpu.VMEM((CHUNK, w), BF16)],
        compiler_params=_cparams(("parallel", "arbitrary")),
        name="hgrn2",
    )(z3, z3, z3, z3, lb_raw, gain, tril)


def _log_sigmoid(x):
    return jnp.minimum(x, 0.0) - jnp.log1p(jnp.exp(-jnp.abs(x)))


def _mlstm_kernel(qk_ref, v_ref, o_in_ref, gc_ref, cw_ref, cb_ref, gb_ref, gain_ref, tril_ref,
                  o_ref, c_ref, n_ref, m_ref, tail_ref, ext_s):
    @pl.when(pl.program_id(1) == 0)
    def _():
        c_ref[...] = jnp.zeros_like(c_ref)
        n_ref[...] = jnp.zeros_like(n_ref)
        m_ref[...] = jnp.full_like(m_ref, M_INIT)
        tail_ref[...] = jnp.zeros_like(tail_ref)

    tril = tril_ref[...]
    gain = gain_ref[...]
    cw = cw_ref[...]
    cb = cb_ref[...]
    gb = gb_ref[...]
    rows = lax.broadcasted_iota(I32, (CHUNK, CHUNK), 0)
    cols = lax.broadcasted_iota(I32, (CHUNK, CHUNK), 1)
    causal = rows >= cols
    n_chunks = qk_ref.shape[0] // CHUNK
    hq = H_B * DQK_B
    scale = DQK_B ** -0.5

    def chunk(c, carry):
        r0 = pl.multiple_of(c * CHUNK, CHUNK)
        u = qk_ref[pl.ds(r0, CHUNK), :].astype(F32)
        ext_s[0:8, :] = tail_ref[...]
        ext_s[8:8 + CHUNK, :] = u
        tail_ref[...] = u[CHUNK - 8:CHUNK, :]
        conv = cb + cw[CONV_W - 1:CONV_W, :] * u
        for j in range(CONV_W - 1):
            lag = CONV_W - 1 - j
            conv = conv + cw[j:j + 1, :] * ext_s[8 - lag:8 - lag + CHUNK, :]
        qk = conv * _sigmoid(conv)
        q = qk[:, :hq]
        k = qk[:, hq:] * scale

        g = gc_ref[pl.ds(r0, CHUNK), :] + gb
        fc = _cumsum_rows(tril, _log_sigmoid(g))
        zpad = jnp.zeros((LANES - CHUNK, LANES), F32)
        g_t = jnp.concatenate([g, zpad], axis=0).T
        fc_t = jnp.concatenate([fc, zpad], axis=0).T

        for h in range(H_B):
            ig_c = g[:, h:h + 1]
            fc_c = fc[:, H_B + h:H_B + h + 1]
            ig_r = g_t[h:h + 1, 0:CHUNK]
            fc_r = fc_t[H_B + h:H_B + h + 1, 0:CHUNK]
            g_tot = fc_c[CHUNK - 1:CHUNK, :]
            qh = q[:, h * DQK_B:(h + 1) * DQK_B]
            kh = k[:, h * DQK_B:(h + 1) * DQK_B]
            vh = v_ref[pl.ds(r0, CHUNK), h * DV_B:(h + 1) * DV_B]
            qb = qh.astype(BF16)
            c_st = c_ref[h]
            n_st = n_ref[h]
            m_st = m_ref[h][0:1, 0:1]

            d_log = jnp.where(causal, fc_c - fc_r + ig_r, -jnp.inf)
            a_int = fc_c + m_st
            m_t = jnp.maximum(jnp.max(d_log, axis=-1, keepdims=True), a_int)
            w_in = jnp.exp(d_log - m_t)
            w_state = jnp.exp(a_int - m_t)
            qkw = _dot_nt(qb, kh.astype(BF16)) * w_in
            num = (jnp.dot(qkw.astype(BF16), vh, preferred_element_type=F32)
                   + w_state * jnp.dot(qb, c_st.astype(BF16), preferred_element_type=F32))
            den = (jnp.sum(qkw, axis=-1, keepdims=True)
                   + w_state * jnp.sum(qh * n_st, axis=-1, keepdims=True))
            hh = num / jnp.maximum(jnp.abs(den), jnp.exp(-m_t))

            a_end = g_tot - fc_c + ig_c
            m_loc = jnp.max(a_end, axis=0, keepdims=True)
            kw = kh * jnp.exp(a_end - m_loc)
            c_loc = _dot_tn(kw.astype(BF16), vh)
            n_loc = jnp.sum(kw, axis=0, keepdims=True)
            m_new = jnp.maximum(g_tot + m_st, m_loc)
            s_prev = jnp.exp(g_tot + m_st - m_new)
            s_loc = jnp.exp(m_loc - m_new)
            c_ref[h] = s_prev * c_st + s_loc * c_loc
            n_ref[h] = s_prev * n_st + s_loc * n_loc
            m_ref[h] = jnp.broadcast_to(m_new, m_ref.shape[1:])

            vs = slice(h * DV_B, (h + 1) * DV_B)
            ms = jnp.mean(hh * hh, axis=-1, keepdims=True)
            og = o_in_ref[pl.ds(r0, CHUNK), vs].astype(F32)
            y = hh * lax.rsqrt(ms + EPS) * gain[:, vs] * _sigmoid(og)
            o_ref[pl.ds(r0, CHUNK), vs] = y.astype(BF16)
        return carry

    lax.fori_loop(0, n_chunks, chunk, 0, unroll=2)


def _mlstm(z3, gc3, conv_w, conv_b, gate_bias, gain, tril, ts):
    b, s, _ = z3.shape
    w = H_B * DV_B

    def zspec(col):
        return pl.BlockSpec((None, ts, w), lambda i, j, col=col: (i, j, col))

    def const(shape):
        return pl.BlockSpec(shape, lambda i, j: (0,) * len(shape))

    return pl.pallas_call(
        _mlstm_kernel,
        out_shape=jax.ShapeDtypeStruct((b, s, w), BF16),
        grid=(b, s // ts),
        in_specs=[zspec(4), zspec(5), zspec(6),
                  pl.BlockSpec((None, ts, LANES), lambda i, j: (i, j, 0)),
                  const((CONV_W, w)), const((1, w)), const((1, LANES)), const((1, w)),
                  const((CHUNK, CHUNK))],
        out_specs=pl.BlockSpec((None, ts, w), lambda i, j: (i, j, 0)),
        scratch_shapes=[pltpu.VMEM((H_B, DQK_B, DV_B), F32),
                        pltpu.VMEM((H_B, 1, DQK_B), F32),
                        pltpu.VMEM((H_B, 8, LANES), F32),
                        pltpu.VMEM((8, w), F32),
                        pltpu.VMEM((8 + CHUNK, w), F32)],
        compiler_params=_cparams(("parallel", "arbitrary")),
        name="mlstm",
    )(z3, z3, z3, gc3, conv_w, conv_b, gate_bias, gain, tril)


def _merge_kernel(ya_ref, yb_ref, ra0_ref, ra1_ref, rb0_ref, rb1_ref, x_ref,
                  wa_ref, wb_ref, wo_ref, g_ref, rw_ref, rb_ref, upper_ref,
                  x1_ref, hp_ref, idx_ref, gate_ref, gcol_ref, rank_ref, cnt_ref, carry_ref):
    @pl.when(pl.program_id(0) == 0)
    def _():
        carry_ref[...] = jnp.zeros_like(carry_ref)

    half = wo_ref.shape[0] // 2
    pa = jnp.dot(ya_ref[...], wa_ref[...], preferred_element_type=F32)
    pb = jnp.dot(yb_ref[...], wb_ref[...], preferred_element_type=F32)
    mix0 = (_sigmoid(ra0_ref[...].astype(F32)) * pa[:, :half]
            + _sigmoid(rb0_ref[...].astype(F32)) * pb[:, :half]).astype(BF16)
    mix1 = (_sigmoid(ra1_ref[...].astype(F32)) * pa[:, half:]
            + _sigmoid(rb1_ref[...].astype(F32)) * pb[:, half:]).astype(BF16)
    x1 = (x_ref[...] + jnp.dot(mix0, wo_ref[0:half, :], preferred_element_type=F32)
          + jnp.dot(mix1, wo_ref[half:, :], preferred_element_type=F32))
    x1_ref[...] = x1

    ms = jnp.mean(x1 * x1, axis=-1, keepdims=True)
    h2 = x1 * lax.rsqrt(ms + EPS) * g_ref[...]
    hb = h2.astype(BF16)
    lo = pltpu.bitcast(hb[:, :half].astype(F32), U32) >> 16
    hi = pltpu.bitcast(hb[:, half:].astype(F32), U32) & jnp.uint32(0xFFFF0000)
    hp_ref[...] = lo | hi

    h_lo = (h2 - hb.astype(F32)).astype(BF16)
    rw = rw_ref[...]
    rw_hi = rw.astype(BF16)
    rw_lo = (rw - rw_hi.astype(F32)).astype(BF16)
    logits = (_dot_nt(rw_hi, hb) + _dot_nt(rw_hi, h_lo) + _dot_nt(rw_lo, hb)) + rb_ref[...]

    tm = logits.shape[1]
    erow = lax.broadcasted_iota(I32, (N_EXPERTS, tm), 0)
    work = logits
    vals, idxs = [], []
    for _ in range(TOP_K):
        mval = jnp.max(work, axis=0, keepdims=True)
        idx = jnp.min(jnp.where(work == mval, erow, N_EXPERTS), axis=0, keepdims=True)
        work = jnp.where(erow == idx, -jnp.inf, work)
        vals.append(mval)
        idxs.append(idx)
    exps = [jnp.exp(v - vals[0]) for v in vals]
    denom = exps[0] + exps[1] + exps[2] + exps[3]
    gates = [e / denom for e in exps]

    onehots = [(erow == idx).astype(F32) for idx in idxs]
    mask = onehots[0] + onehots[1] + onehots[2] + onehots[3]
    before = jnp.dot(mask.astype(BF16), upper_ref[...], preferred_element_type=F32)
    before = before + carry_ref[:, 0:1]
    carry = carry_ref[...] + jnp.sum(mask, axis=1, keepdims=True)
    carry_ref[...] = carry
    cnt_ref[...] = carry

    idx_ref[...] = jnp.concatenate(idxs, axis=0)
    gate_rows = jnp.concatenate(gates, axis=0)
    gate_ref[...] = gate_rows
    rank_ref[...] = jnp.concatenate(
        [jnp.sum(oh * before, axis=0, keepdims=True) for oh in onehots], axis=0).astype(I32)
    padded = jnp.concatenate([gate_rows, jnp.zeros((LANES - TOP_K, tm), F32)], axis=0)
    gcol_ref[...] = padded.T


def _merge(ya, yb, z, x2d, wa, wb, wo, gain, rw_t, rb_col, upper, tm):
    t, d = x2d.shape
    half = d // 2
    ra_blk = (4 * H_A * DK_A + 3 * H_B * DV_B) // half

    def rows(width, col=0):
        return pl.BlockSpec((tm, width), lambda i, col=col: (i, col))

    def const(shape):
        return pl.BlockSpec(shape, lambda i: (0,) * len(shape), pipeline_mode=pl.Buffered(1))

    out_shape = (
        jax.ShapeDtypeStruct((t, d), F32),
        jax.ShapeDtypeStruct((t, half), U32),
        jax.ShapeDtypeStruct((TOP_K, t), I32),
        jax.ShapeDtypeStruct((TOP_K, t), F32),
        jax.ShapeDtypeStruct((t, LANES), F32),
        jax.ShapeDtypeStruct((TOP_K, t), I32),
        jax.ShapeDtypeStruct((N_EXPERTS, LANES), F32),
    )
    return pl.pallas_call(
        _merge_kernel,
        out_shape=out_shape,
        grid=(t // tm,),
        in_specs=[rows(ya.shape[1]), rows(yb.shape[1]),
                  rows(half, ra_blk), rows(half, ra_blk + 1), rows(half, ra_blk + 2),
                  rows(half, ra_blk + 3), rows(d),
                  const(wa.shape), const(wb.shape), const(wo.shape), const((1, d)),
                  const(rw_t.shape), const((N_EXPERTS, 1)), const((tm, tm))],
        out_specs=(rows(d), rows(half),
                   pl.BlockSpec((TOP_K, tm), lambda i: (0, i)),
                   pl.BlockSpec((TOP_K, tm), lambda i: (0, i)),
                   rows(LANES),
                   pl.BlockSpec((TOP_K, tm), lambda i: (0, i)),
                   pl.BlockSpec((N_EXPERTS, LANES), lambda i: (0, 0))),
        scratch_shapes=[pltpu.VMEM((N_EXPERTS, LANES), F32)],
        compiler_params=_cparams(("arbitrary",)),
        name="merge_router",
    )(ya, yb, z, z, z, z, x2d, wa, wb, wo, gain, rw_t, rb_col, upper)


def _dispatch_kernel(pad_end_ref, padded_ref, pos_ref, src_ref, dst_ref, zero_s, zsem, sem):
    td = src_ref.shape[0]

    @pl.when(pl.program_id(0) == 0)
    def _():
        zero_s[...] = jnp.zeros_like(zero_s)

        def tail_copy(e):
            start = pl.multiple_of(pad_end_ref[e] - EXPERT_TILE, EXPERT_TILE)
            return pltpu.make_async_copy(zero_s, dst_ref.at[pl.ds(start, EXPERT_TILE)], zsem)

        for e in range(N_EXPERTS):
            @pl.when(padded_ref[e] > 0)
            def _():
                tail_copy(e).start()
        for e in range(N_EXPERTS):
            @pl.when(padded_ref[e] > 0)
            def _():
                tail_copy(e).wait()

    def issue(r, carry):
        for k in range(TOP_K):
            pltpu.make_async_copy(src_ref.at[pl.ds(r, 1)],
                                  dst_ref.at[pl.ds(pos_ref[k, r], 1)], sem).start()
        return carry

    lax.fori_loop(0, td, issue, 0, unroll=8)
    for k in range(TOP_K):
        pltpu.make_async_copy(src_ref, dst_ref.at[pl.ds(0, td)], sem).wait()


def _dispatch(pad_end, padded, pos, src, n_rows, td):
    t, d = src.shape
    grid_spec = pltpu.PrefetchScalarGridSpec(
        num_scalar_prefetch=2,
        grid=(t // td,),
        in_specs=[pl.BlockSpec((TOP_K, td), lambda i, pe, pd: (0, i), memory_space=pltpu.SMEM),
                  pl.BlockSpec((td, d), lambda i, pe, pd: (i, 0))],
        out_specs=pl.BlockSpec(memory_space=pl.ANY),
        scratch_shapes=[pltpu.VMEM((EXPERT_TILE, d), src.dtype),
                        pltpu.SemaphoreType.DMA, pltpu.SemaphoreType.DMA],
    )
    return pl.pallas_call(
        _dispatch_kernel,
        out_shape=jax.ShapeDtypeStruct((n_rows, d), src.dtype),
        grid_spec=grid_spec,
        compiler_params=pltpu.CompilerParams(dimension_semantics=("arbitrary",),
                                             vmem_limit_bytes=V7X_VMEM_LIMIT_BYTES,
                                             has_side_effects=True),
        name="dispatch",
    )(pad_end, padded, pos, src)


def _expert_kernel(te_ref, tv_ref, xs_ref, wgu_hbm, wdn_hbm, bgu_ref, bdn_ref, o_ref,
                   wgu_s, wdn_s, stage_s, sems, xb_s, act_s, cur_ref):
    i = pl.program_id(0)
    valid = tv_ref[i] > 0
    e = te_ref[i]
    d, two_ff = wgu_s.shape
    d_ff = two_ff // 2
    half = d // 2
    n_gu = two_ff // WEIGHT_COLS
    n_all = n_gu + d // WEIGHT_COLS

    @pl.when(i == 0)
    def _():
        cur_ref[0] = -1

    @pl.when(jnp.logical_and(valid, cur_ref[0] != e))
    def _():
        def chunk_copy(j):
            if j < n_gu:
                src = wgu_hbm.at[e, :, pl.ds(j * WEIGHT_COLS, WEIGHT_COLS)]
            else:
                src = wdn_hbm.at[e, :, pl.ds((j - n_gu) * WEIGHT_COLS, WEIGHT_COLS)]
            return pltpu.make_async_copy(src, stage_s.at[j % 2], sems.at[j % 2])

        chunk_copy(0).start()
        for j in range(n_all):
            if j + 1 < n_all:
                chunk_copy(j + 1).start()
            chunk_copy(j).wait()
            w = stage_s[j % 2].astype(BF16)
            if j < n_gu:
                wgu_s[:, j * WEIGHT_COLS:(j + 1) * WEIGHT_COLS] = w
            else:
                wdn_s[:, (j - n_gu) * WEIGHT_COLS:(j - n_gu + 1) * WEIGHT_COLS] = w
        cur_ref[0] = e

    @pl.when(valid)
    def _():
        w = xs_ref[...]
        xb_s[:, 0:half] = pltpu.bitcast(w << 16, F32).astype(BF16)
        xb_s[:, half:] = pltpu.bitcast(w & jnp.uint32(0xFFFF0000), F32).astype(BF16)
        xb = xb_s[...]
        for c in range(d_ff // FF_CHUNK):
            ca = slice(c * FF_CHUNK, (c + 1) * FF_CHUNK)
            cu = slice(d_ff + c * FF_CHUNK, d_ff + (c + 1) * FF_CHUNK)
            a = jnp.dot(xb, wgu_s[:, ca], preferred_element_type=F32) + bgu_ref[:, ca]
            u = jnp.dot(xb, wgu_s[:, cu], preferred_element_type=F32) + bgu_ref[:, cu]
            a = jnp.minimum(a, SWIGLU_LIMIT)
            u = jnp.clip(u, -SWIGLU_LIMIT, SWIGLU_LIMIT)
            act_s[:, ca] = ((u + 1.0) * a * _sigmoid(SWIGLU_ALPHA * a)).astype(BF16)
        o_ref[...] = jnp.dot(act_s[...], wdn_s[...], preferred_element_type=F32) + bdn_ref[...]

    @pl.when(jnp.logical_not(valid))
    def _():
        o_ref[...] = jnp.zeros_like(o_ref)


def _experts(tile_expert, tile_valid, xs, w_gu, b_gu, w_dn, b_dn):
    n_rows, half = xs.shape
    d = 2 * half
    d_ff = w_dn.shape[1]
    n_tiles = n_rows // EXPERT_TILE

    grid_spec = pltpu.PrefetchScalarGridSpec(
        num_scalar_prefetch=2,
        grid=(n_tiles,),
        in_specs=[
            pl.BlockSpec((EXPERT_TILE, half), lambda i, te, tv: (jnp.minimum(i, tv[n_tiles]), 0)),
            pl.BlockSpec(memory_space=pl.ANY),
            pl.BlockSpec(memory_space=pl.ANY),
            pl.BlockSpec((None, 1, 2 * d_ff), lambda i, te, tv: (te[i], 0, 0)),
            pl.BlockSpec((None, 1, d), lambda i, te, tv: (te[i], 0, 0)),
        ],
        out_specs=pl.BlockSpec((EXPERT_TILE, d), lambda i, te, tv: (i, 0)),
        scratch_shapes=[pltpu.VMEM((d, 2 * d_ff), BF16),
                        pltpu.VMEM((d_ff, d), BF16),
                        pltpu.VMEM((2, d, WEIGHT_COLS), F32),
                        pltpu.SemaphoreType.DMA((2,)),
                        pltpu.VMEM((EXPERT_TILE, d), BF16),
                        pltpu.VMEM((EXPERT_TILE, d_ff), BF16),
                        pltpu.SMEM((1,), I32)],
    )
    assert d == d_ff, "weight staging shares one (d, WEIGHT_COLS) buffer for w_gu and w_dn chunks"
    return pl.pallas_call(
        _expert_kernel,
        out_shape=jax.ShapeDtypeStruct((n_rows, d), F32),
        grid_spec=grid_spec,
        compiler_params=_cparams(("arbitrary",)),
        name="experts",
    )(tile_expert, tile_valid, xs, w_gu, w_dn, b_gu, b_dn)


def _combine_kernel(pos_ref, posn_ref, ys_ref, x1_ref, gcol_ref, p_ref, g_ple_ref, wpg_ref,
                    wple_ref, g_fin_ref, o_ref, buf, sems):
    tc = x1_ref.shape[0]
    i = pl.program_id(0)
    slot = lax.rem(i, 2)

    def gather(idx_ref, dst_slot):
        def issue(r, carry):
            for k in range(TOP_K):
                pltpu.make_async_copy(ys_ref.at[pl.ds(idx_ref[k, r], 1)],
                                      buf.at[dst_slot, k, pl.ds(r, 1)], sems.at[dst_slot]).start()
            return carry
        lax.fori_loop(0, tc, issue, 0, unroll=True)

    @pl.when(i == 0)
    def _():
        gather(pos_ref, slot)

    gather(posn_ref, 1 - slot)

    def wait_slot(s):
        for k in range(TOP_K):
            pltpu.make_async_copy(ys_ref.at[pl.ds(0, tc)], buf.at[s, k], sems.at[s]).wait()

    ple = jnp.dot(p_ref[...].astype(BF16), wple_ref[...], preferred_element_type=F32)
    wait_slot(slot)

    @pl.when(i + 1 == pl.num_programs(0))
    def _():
        wait_slot(1 - slot)

    gcol = gcol_ref[...]
    moe = buf[slot, 0] * gcol[:, 0:1]
    for k in range(1, TOP_K):
        moe = moe + buf[slot, k] * gcol[:, k:k + 1]
    x2 = x1_ref[...] + moe
    ms = jnp.mean(x2 * x2, axis=-1, keepdims=True)
    h3 = (x2 * lax.rsqrt(ms + EPS) * g_ple_ref[...]).astype(BF16)
    gate = _sigmoid(jnp.dot(h3, wpg_ref[...], preferred_element_type=F32))
    x3 = x2 + ple * gate
    ms3 = jnp.mean(x3 * x3, axis=-1, keepdims=True)
    o_ref[...] = x3 * lax.rsqrt(ms3 + EPS) * g_fin_ref[...]


def _combine(pos, ys, x1, gcol, p2d, g_ple, wpg, wple, g_fin, tc):
    t, d = x1.shape
    n_steps = t // tc

    def rows(width):
        return pl.BlockSpec((tc, width), lambda i: (i, 0))

    def const(shape):
        return pl.BlockSpec(shape, lambda i: (0,) * len(shape), pipeline_mode=pl.Buffered(1))

    return pl.pallas_call(
        _combine_kernel,
        out_shape=jax.ShapeDtypeStruct((t, d), F32),
        grid=(t // tc,),
        in_specs=[pl.BlockSpec((TOP_K, tc), lambda i: (0, i), memory_space=pltpu.SMEM),
                  pl.BlockSpec((TOP_K, tc), lambda i: (0, jnp.minimum(i + 1, n_steps - 1)),
                               memory_space=pltpu.SMEM),
                  pl.BlockSpec(memory_space=pl.ANY),
                  rows(d), rows(LANES), rows(p2d.shape[1]),
                  const((1, d)), const(wpg.shape), const(wple.shape), const((1, d))],
        out_specs=rows(d),
        scratch_shapes=[pltpu.VMEM((2, TOP_K, tc, d), F32), pltpu.SemaphoreType.DMA((2,))],
        compiler_params=_cparams(("arbitrary",)),
        name="combine_ple",
    )(pos, pos, ys, x1, gcol, p2d, g_ple, wpg, wple, g_fin)


def _pick(n, pref):
    return pref if n % pref == 0 else n


def kernel(x, p, norm_mix, w_in, hgrn_lb, hgrn_norm, mlstm_conv_w, mlstm_conv_b, mlstm_b_i,
           mlstm_b_f, mlstm_norm, w_branch_a, w_branch_b, w_out, norm_moe, router_w, router_b,
           exp_w_gu, exp_b_gu, exp_w_dn, exp_b_dn, norm_ple, w_ple, w_ple_gate, norm_final):
    b, s, d = x.shape
    t = b * s
    depth = norm_mix.shape[0]
    wa_cols = H_A * DK_A
    wb_cols = H_B * DV_B
    g0 = 4 * wa_cols + 2 * wb_cols
    g1 = g0 + 2 * H_B
    tril = jnp.tril(jnp.ones((CHUNK, CHUNK), F32)).astype(BF16)
    x2d = x.reshape(t, d)

    for i in range(depth):
        w_i = w_in[i]
        w_main = jnp.concatenate([w_i[:, :g0], w_i[:, g1:]], axis=1).astype(BF16)
        w_gate = jnp.pad(w_i[:, g0:g1], ((0, 0), (0, LANES - 2 * H_B))).astype(BF16)
        gate_bias = jnp.pad(jnp.concatenate([mlstm_b_i[i], mlstm_b_f[i]]),
                            (0, LANES - 2 * H_B)).reshape(1, LANES)

        tm_in = _pick(t, 1024)
        tn_in = _pick(w_main.shape[1], 1408)
        z, gc = _inproj(x2d, norm_mix[i].reshape(1, d), w_main, w_gate, tm_in, tn_in)
        z3 = z.reshape(b, s, z.shape[1])
        gc3 = gc.reshape(b, s, LANES)

        ts = _pick(s, 1024)
        ya = _hgrn(z3, hgrn_lb[i:i + 2], hgrn_norm[i].reshape(1, wa_cols), tril, ts)
        yb = _mlstm(z3, gc3, mlstm_conv_w[i], mlstm_conv_b[i].reshape(1, -1), gate_bias,
                    mlstm_norm[i].reshape(1, wb_cols), tril, ts)

        tm_mg = _pick(t, 512)
        upper = jnp.triu(jnp.ones((tm_mg, tm_mg), F32), 1).astype(BF16)
        x1, hp, idx, gate_rows, gcol, rank, cnt = _merge(
            ya.reshape(t, wa_cols), yb.reshape(t, wb_cols), z, x2d,
            w_branch_a[i].astype(BF16), w_branch_b[i].astype(BF16), w_out[i].astype(BF16),
            norm_moe[i].reshape(1, d), router_w[i].T, router_b[i].reshape(N_EXPERTS, 1),
            upper, tm_mg)
        del gate_rows

        counts = cnt[:, 0].astype(I32)
        padded = (counts + EXPERT_TILE - 1) // EXPERT_TILE * EXPERT_TILE
        pad_end = jnp.cumsum(padded)
        pad_start = pad_end - padded
        e_ids = jnp.arange(N_EXPERTS, dtype=I32)
        pos = rank + jnp.sum(jnp.where(idx[..., None] == e_ids, pad_start, 0), axis=-1)
        n_rows = t * TOP_K + N_EXPERTS * EXPERT_TILE
        n_tiles = n_rows // EXPERT_TILE
        tile_row = jnp.arange(n_tiles, dtype=I32) * EXPERT_TILE
        n_valid = pad_end[-1] // EXPERT_TILE
        last_valid = jnp.maximum(n_valid - 1, 0)
        tile_valid = (tile_row < pad_end[-1]).astype(I32)
        tile_expert = jnp.sum((tile_row[:, None] >= pad_end[None, :]).astype(I32), axis=1)
        tile_expert = jnp.minimum(tile_expert, N_EXPERTS - 1)
        last_e = jnp.sum(jnp.where(jnp.arange(n_tiles) == last_valid, tile_expert, 0))
        tile_expert = jnp.where(tile_valid > 0, tile_expert, last_e).astype(I32)
        tile_valid = jnp.concatenate([tile_valid, last_valid.reshape(1).astype(I32)])

        xs = _dispatch(pad_end.astype(I32), padded, pos, hp, n_rows, _pick(t, 512))
        ys = _experts(tile_expert, tile_valid, xs,
                      exp_w_gu[i], exp_b_gu[i].reshape(N_EXPERTS, 1, -1),
                      exp_w_dn[i], exp_b_dn[i].reshape(N_EXPERTS, 1, d))

        is_last = i == depth - 1
        assert is_last, "the fused final-norm epilogue assumes a single layer"
        x2d = _combine(pos, ys, x1, gcol, p[i].reshape(t, -1), norm_ple[i].reshape(1, d),
                       w_ple_gate[i].astype(BF16), w_ple[i].astype(BF16),
                       norm_final.reshape(1, d), _pick(t, 256))
    return x2d.reshape(b, s, d)
```

```python
import functools

import jax
import jax.numpy as jnp
from jax import lax
from jax.experimental import pallas as pl
from jax.experimental.pallas import tpu as pltpu

F32 = jnp.float32
BF16 = jnp.bfloat16
U32 = jnp.uint32
I32 = jnp.int32

EPS = 1e-6
CHUNK = 64
H_A, DK_A, DV_A = 8, 128, 128
H_B, DQK_B, DV_B = 4, 128, 256
CONV_W = 4
M_INIT = -1e30
N_EXPERTS = 32
TOP_K = 4
SWIGLU_LIMIT = 7.0
SWIGLU_ALPHA = 1.702

V7X_VMEM_LIMIT_BYTES = 56 * 1024 * 1024
LANES = 128

EXPERT_TILE = 512
FF_CHUNK = 512
WEIGHT_ROWS = 256
WEIGHT_RING = 4


def _cparams(sem):
    return pltpu.CompilerParams(dimension_semantics=sem, vmem_limit_bytes=V7X_VMEM_LIMIT_BYTES)


def _sigmoid(x):
    return jax.nn.sigmoid(x)


def _split3(x):
    hi = x.astype(BF16)
    r1 = x - hi.astype(F32)
    mid = r1.astype(BF16)
    lo = (r1 - mid.astype(F32)).astype(BF16)
    return hi, mid, lo


def _cumsum_rows(tril, x):
    hi, mid, lo = _split3(x)
    acc = jnp.dot(tril, hi, preferred_element_type=F32)
    acc = acc + jnp.dot(tril, mid, preferred_element_type=F32)
    return acc + jnp.dot(tril, lo, preferred_element_type=F32)


def _dot_tn(a, b):
    return lax.dot_general(a, b, (((0,), (0,)), ((), ())), preferred_element_type=F32)


def _dot_nt(a, b):
    return lax.dot_general(a, b, (((1,), (1,)), ((), ())), preferred_element_type=F32)


def _inproj_kernel(x_ref, g_ref, w_ref, wg_ref, z_ref, gc_ref, h_scr):
    @pl.when(pl.program_id(1) == 0)
    def _():
        x = x_ref[...]
        ms = jnp.mean(x * x, axis=-1, keepdims=True)
        hb = (x * lax.rsqrt(ms + EPS) * g_ref[...]).astype(BF16)
        h_scr[...] = hb
        gc_ref[...] = jnp.dot(hb, wg_ref[...], preferred_element_type=F32)

    z_ref[...] = jnp.dot(h_scr[...], w_ref[...], preferred_element_type=F32).astype(BF16)


def _inproj(x2d, gain, w_main, w_gate, tm, tn):
    t, d = x2d.shape
    n = w_main.shape[1]
    return pl.pallas_call(
        _inproj_kernel,
        out_shape=(jax.ShapeDtypeStruct((t, n), BF16), jax.ShapeDtypeStruct((t, LANES), F32)),
        grid=(t // tm, n // tn),
        in_specs=[
            pl.BlockSpec((tm, d), lambda i, j: (i, 0)),
            pl.BlockSpec((1, d), lambda i, j: (0, 0)),
            pl.BlockSpec((d, tn), lambda i, j: (0, j)),
            pl.BlockSpec((d, LANES), lambda i, j: (0, 0)),
        ],
        out_specs=(
            pl.BlockSpec((tm, tn), lambda i, j: (i, j)),
            pl.BlockSpec((tm, LANES), lambda i, j: (i, 0)),
        ),
        scratch_shapes=[pltpu.VMEM((tm, d), BF16)],
        compiler_params=_cparams(("parallel", "arbitrary")),
        name="inproj",
    )(x2d, gain, w_main, w_gate)


def _hgrn_chunk_fn(q_ref, f_ref, v_ref, g_ref, lb_ref, gain_ref, tril_ref, o_ref, st_ref):
    lbr = lb_ref[...]
    mx = jnp.maximum(lbr[0:1], lbr[1:2])
    e0 = jnp.exp(lbr[0:1] - mx)
    e1 = jnp.exp(lbr[1:2] - mx)
    lb = e0 / (e0 + e1)
    one_m_lb = 1.0 - lb
    tril = tril_ref[...]
    gain = gain_ref[...]
    rows = lax.broadcasted_iota(I32, (CHUNK, CHUNK), 0)
    cols = lax.broadcasted_iota(I32, (CHUNK, CHUNK), 1)
    causal = rows >= cols

    def chunk(c):
        r0 = pl.multiple_of(c * CHUNK, CHUNK)
        q = q_ref[pl.ds(r0, CHUNK), :].astype(F32)
        fl = f_ref[pl.ds(r0, CHUNK), :].astype(F32)
        log_f = jnp.log(lb + one_m_lb * _sigmoid(fl))
        k = one_m_lb * _sigmoid(-fl)
        qs = q * _sigmoid(q)
        bc = _cumsum_rows(tril, log_f)
        b_ref = bc[CHUNK // 2:CHUNK // 2 + 1, :]
        b_end = bc[CHUNK - 1:CHUNK, :]
        qd = (qs * jnp.exp(bc - b_ref)).astype(BF16)
        kd = (k * jnp.exp(b_ref - bc)).astype(BF16)
        ke = (k * jnp.exp(b_end - bc)).astype(BF16)
        qe = (qs * jnp.exp(bc)).astype(BF16)
        decay = jnp.exp(b_end)
        for h in range(H_A):
            sl = slice(h * DK_A, (h + 1) * DK_A)
            scores = _dot_nt(qd[:, sl], kd[:, sl])
            scores = jnp.where(causal, scores, 0.0).astype(BF16)
            vh = v_ref[pl.ds(r0, CHUNK), sl]
            st = st_ref[h]
            o = jnp.dot(scores, vh, preferred_element_type=F32)
            o = o + _dot_nt(qe[:, sl], st.astype(BF16))
            u_t = _dot_tn(vh, ke[:, sl])
            st_ref[h] = st * decay[:, sl] + u_t
            ms = jnp.mean(o * o, axis=-1, keepdims=True)
            gt = g_ref[pl.ds(r0, CHUNK), sl].astype(F32)
            y = o * lax.rsqrt(ms + EPS) * gain[:, sl] * (gt * _sigmoid(gt))
            o_ref[pl.ds(r0, CHUNK), sl] = y.astype(BF16)

    return chunk


def _log_sigmoid(x):
    return jnp.minimum(x, 0.0) - jnp.log1p(jnp.exp(-jnp.abs(x)))


def _mlstm_chunk_fn(qk_ref, v_ref, o_in_ref, gc_ref, cw_ref, cb_ref, gb_ref, gain_ref, tril_ref,
                    o_ref, c_ref, n_ref, m_ref, tail_ref, ext_s):
    tril = tril_ref[...]
    gain = gain_ref[...]
    cw = cw_ref[...]
    cb = cb_ref[...]
    gb = gb_ref[...]
    rows = lax.broadcasted_iota(I32, (CHUNK, CHUNK), 0)
    cols = lax.broadcasted_iota(I32, (CHUNK, CHUNK), 1)
    causal = rows >= cols
    hq = H_B * DQK_B
    scale = DQK_B ** -0.5

    def chunk(c):
        r0 = pl.multiple_of(c * CHUNK, CHUNK)
        u = qk_ref[pl.ds(r0, CHUNK), :].astype(F32)
        ext_s[0:8, :] = tail_ref[...]
        ext_s[8:8 + CHUNK, :] = u
        tail_ref[...] = u[CHUNK - 8:CHUNK, :]
        conv = cb + cw[CONV_W - 1:CONV_W, :] * u
        for j in range(CONV_W - 1):
            lag = CONV_W - 1 - j
            conv = conv + cw[j:j + 1, :] * ext_s[8 - lag:8 - lag + CHUNK, :]
        qk = conv * _sigmoid(conv)
        q = qk[:, :hq]
        k = qk[:, hq:] * scale

        g = gc_ref[pl.ds(r0, CHUNK), :] + gb
        fc = _cumsum_rows(tril, _log_sigmoid(g))
        zpad = jnp.zeros((LANES - CHUNK, LANES), F32)
        g_t = jnp.concatenate([g, zpad], axis=0).T
        fc_t = jnp.concatenate([fc, zpad], axis=0).T

        for h in range(H_B):
            ig_c = g[:, h:h + 1]
            fc_c = fc[:, H_B + h:H_B + h + 1]
            ig_r = g_t[h:h + 1, 0:CHUNK]
            fc_r = fc_t[H_B + h:H_B + h + 1, 0:CHUNK]
            g_tot = fc_c[CHUNK - 1:CHUNK, :]
            qh = q[:, h * DQK_B:(h + 1) * DQK_B]
            kh = k[:, h * DQK_B:(h + 1) * DQK_B]
            vh = v_ref[pl.ds(r0, CHUNK), h * DV_B:(h + 1) * DV_B]
            qb = qh.astype(BF16)
            c_st = c_ref[h]
            n_st = n_ref[h]
            m_st = m_ref[h][0:1, 0:1]

            d_log = jnp.where(causal, fc_c - fc_r + ig_r, -jnp.inf)
            a_int = fc_c + m_st
            m_t = jnp.maximum(jnp.max(d_log, axis=-1, keepdims=True), a_int)
            w_in = jnp.exp(d_log - m_t)
            w_state = jnp.exp(a_int - m_t)
            qkw = _dot_nt(qb, kh.astype(BF16)) * w_in
            num = (jnp.dot(qkw.astype(BF16), vh, preferred_element_type=F32)
                   + w_state * jnp.dot(qb, c_st.astype(BF16), preferred_element_type=F32))
            den = (jnp.sum(qkw, axis=-1, keepdims=True)
                   + w_state * jnp.sum(qh * n_st, axis=-1, keepdims=True))
            hh = num / jnp.maximum(jnp.abs(den), jnp.exp(-m_t))

            a_end = g_tot - fc_c + ig_c
            m_loc = jnp.max(a_end, axis=0, keepdims=True)
            kw = kh * jnp.exp(a_end - m_loc)
            c_loc = _dot_tn(kw.astype(BF16), vh)
            n_loc = jnp.sum(kw, axis=0, keepdims=True)
            m_new = jnp.maximum(g_tot + m_st, m_loc)
            s_prev = jnp.exp(g_tot + m_st - m_new)
            s_loc = jnp.exp(m_loc - m_new)
            c_ref[h] = s_prev * c_st + s_loc * c_loc
            n_ref[h] = s_prev * n_st + s_loc * n_loc
            m_ref[h] = jnp.broadcast_to(m_new, m_ref.shape[1:])

            vs = slice(h * DV_B, (h + 1) * DV_B)
            ms = jnp.mean(hh * hh, axis=-1, keepdims=True)
            og = o_in_ref[pl.ds(r0, CHUNK), vs].astype(F32)
            y = hh * lax.rsqrt(ms + EPS) * gain[:, vs] * _sigmoid(og)
            o_ref[pl.ds(r0, CHUNK), vs] = y.astype(BF16)

    return chunk


def _mixers_kernel(q_ref, f_ref, va_ref, g_ref, lb_ref, gain_a_ref,
                   qk_ref, vb_ref, ob_ref, gc_ref, cw_ref, cb_ref, gb_ref, gain_b_ref, tril_ref,
                   ya_ref, yb_ref, st_ref, c_ref, n_ref, m_ref, tail_ref, ext_s):
    @pl.when(pl.program_id(1) == 0)
    def _():
        st_ref[...] = jnp.zeros_like(st_ref)
        c_ref[...] = jnp.zeros_like(c_ref)
        n_ref[...] = jnp.zeros_like(n_ref)
        m_ref[...] = jnp.full_like(m_ref, M_INIT)
        tail_ref[...] = jnp.zeros_like(tail_ref)

    hgrn_chunk = _hgrn_chunk_fn(q_ref, f_ref, va_ref, g_ref, lb_ref, gain_a_ref, tril_ref,
                                ya_ref, st_ref)
    mlstm_chunk = _mlstm_chunk_fn(qk_ref, vb_ref, ob_ref, gc_ref, cw_ref, cb_ref, gb_ref,
                                  gain_b_ref, tril_ref, yb_ref, c_ref, n_ref, m_ref, tail_ref,
                                  ext_s)

    def body(c, carry):
        hgrn_chunk(c)
        mlstm_chunk(c)
        return carry

    lax.fori_loop(0, q_ref.shape[0] // CHUNK, body, 0)


def _mixers(z3, gc3, lb_raw, gain_a, conv_w, conv_b, gate_bias, gain_b, tril, ts):
    b, s, _ = z3.shape
    w = H_A * DK_A
    assert w == H_B * DV_B == 2 * H_B * DQK_B, "column blocks of z share one width"

    def zspec(col):
        return pl.BlockSpec((None, ts, w), lambda i, j, col=col: (i, j, col))

    def const(shape):
        return pl.BlockSpec(shape, lambda i, j: (0,) * len(shape))

    out = jax.ShapeDtypeStruct((b, s, w), BF16)
    out_spec = pl.BlockSpec((None, ts, w), lambda i, j: (i, j, 0))
    return pl.pallas_call(
        _mixers_kernel,
        out_shape=(out, out),
        grid=(b, s // ts),
        in_specs=[zspec(0), zspec(1), zspec(2), zspec(3), const((2, w)), const((1, w)),
                  zspec(4), zspec(5), zspec(6),
                  pl.BlockSpec((None, ts, LANES), lambda i, j: (i, j, 0)),
                  const((CONV_W, w)), const((1, w)), const((1, LANES)), const((1, w)),
                  const((CHUNK, CHUNK))],
        out_specs=(out_spec, out_spec),
        scratch_shapes=[pltpu.VMEM((H_A, DV_A, DK_A), F32),
                        pltpu.VMEM((H_B, DQK_B, DV_B), F32),
                        pltpu.VMEM((H_B, 1, DQK_B), F32),
                        pltpu.VMEM((H_B, 8, LANES), F32),
                        pltpu.VMEM((8, w), F32),
                        pltpu.VMEM((8 + CHUNK, w), F32)],
        compiler_params=_cparams(("parallel", "arbitrary")),
        name="mixers",
    )(z3, z3, z3, z3, lb_raw, gain_a, z3, z3, z3, gc3, conv_w, conv_b, gate_bias, gain_b, tril)


def _merge_kernel(ya_ref, yb_ref, ra0_ref, ra1_ref, rb0_ref, rb1_ref, x_ref,
                  wa_ref, wb_ref, wo_ref, g_ref, rw_ref, rb_ref, upper_ref,
                  x1_ref, hp_ref, idx_ref, gate_ref, gcol_ref, rank_ref, cnt_ref, carry_ref):
    @pl.when(pl.program_id(0) == 0)
    def _():
        carry_ref[...] = jnp.zeros_like(carry_ref)

    half = wo_ref.shape[0] // 2
    pa = jnp.dot(ya_ref[...], wa_ref[...], preferred_element_type=F32)
    pb = jnp.dot(yb_ref[...], wb_ref[...], preferred_element_type=F32)
    mix0 = (_sigmoid(ra0_ref[...].astype(F32)) * pa[:, :half]
            + _sigmoid(rb0_ref[...].astype(F32)) * pb[:, :half]).astype(BF16)
    mix1 = (_sigmoid(ra1_ref[...].astype(F32)) * pa[:, half:]
            + _sigmoid(rb1_ref[...].astype(F32)) * pb[:, half:]).astype(BF16)
    x1 = (x_ref[...] + jnp.dot(mix0, wo_ref[0:half, :], preferred_element_type=F32)
          + jnp.dot(mix1, wo_ref[half:, :], preferred_element_type=F32))
    x1_ref[...] = x1

    ms = jnp.mean(x1 * x1, axis=-1, keepdims=True)
    h2 = x1 * lax.rsqrt(ms + EPS) * g_ref[...]
    hb = h2.astype(BF16)
    lo = pltpu.bitcast(hb[:, :half].astype(F32), U32) >> 16
    hi = pltpu.bitcast(hb[:, half:].astype(F32), U32) & jnp.uint32(0xFFFF0000)
    hp_ref[...] = lo | hi

    h_lo = (h2 - hb.astype(F32)).astype(BF16)
    rw = rw_ref[...]
    rw_hi = rw.astype(BF16)
    rw_lo = (rw - rw_hi.astype(F32)).astype(BF16)
    logits = (_dot_nt(rw_hi, hb) + _dot_nt(rw_hi, h_lo) + _dot_nt(rw_lo, hb)) + rb_ref[...]

    tm = logits.shape[1]
    erow = lax.broadcasted_iota(I32, (N_EXPERTS, tm), 0)
    work = logits
    vals, idxs = [], []
    for _ in range(TOP_K):
        mval = jnp.max(work, axis=0, keepdims=True)
        idx = jnp.min(jnp.where(work == mval, erow, N_EXPERTS), axis=0, keepdims=True)
        work = jnp.where(erow == idx, -jnp.inf, work)
        vals.append(mval)
        idxs.append(idx)
    exps = [jnp.exp(v - vals[0]) for v in vals]
    denom = exps[0] + exps[1] + exps[2] + exps[3]
    gates = [e / denom for e in exps]

    onehots = [(erow == idx).astype(F32) for idx in idxs]
    mask = onehots[0] + onehots[1] + onehots[2] + onehots[3]
    before = jnp.dot(mask.astype(BF16), upper_ref[...], preferred_element_type=F32)
    before = before + carry_ref[:, 0:1]
    carry = carry_ref[...] + jnp.sum(mask, axis=1, keepdims=True)
    carry_ref[...] = carry
    cnt_ref[...] = carry

    idx_ref[...] = jnp.concatenate(idxs, axis=0)
    gate_rows = jnp.concatenate(gates, axis=0)
    gate_ref[...] = gate_rows
    rank_ref[...] = jnp.concatenate(
        [jnp.sum(oh * before, axis=0, keepdims=True) for oh in onehots], axis=0).astype(I32)
    padded = jnp.concatenate([gate_rows, jnp.zeros((LANES - TOP_K, tm), F32)], axis=0)
    gcol_ref[...] = padded.T


def _merge(ya, yb, z, x2d, wa, wb, wo, gain, rw_t, rb_col, upper, tm):
    t, d = x2d.shape
    half = d // 2
    ra_blk = (4 * H_A * DK_A + 3 * H_B * DV_B) // half

    def rows(width, col=0):
        return pl.BlockSpec((tm, width), lambda i, col=col: (i, col))

    def const(shape):
        return pl.BlockSpec(shape, lambda i: (0,) * len(shape), pipeline_mode=pl.Buffered(1))

    out_shape = (
        jax.ShapeDtypeStruct((t, d), F32),
        jax.ShapeDtypeStruct((t, half), U32),
        jax.ShapeDtypeStruct((TOP_K, t), I32),
        jax.ShapeDtypeStruct((TOP_K, t), F32),
        jax.ShapeDtypeStruct((t, LANES), F32),
        jax.ShapeDtypeStruct((TOP_K, t), I32),
        jax.ShapeDtypeStruct((N_EXPERTS, LANES), F32),
    )
    return pl.pallas_call(
        _merge_kernel,
        out_shape=out_shape,
        grid=(t // tm,),
        in_specs=[rows(ya.shape[1]), rows(yb.shape[1]),
                  rows(half, ra_blk), rows(half, ra_blk + 1), rows(half, ra_blk + 2),
                  rows(half, ra_blk + 3), rows(d),
                  const(wa.shape), const(wb.shape), const(wo.shape), const((1, d)),
                  const(rw_t.shape), const((N_EXPERTS, 1)), const((tm, tm))],
        out_specs=(rows(d), rows(half),
                   pl.BlockSpec((TOP_K, tm), lambda i: (0, i)),
                   pl.BlockSpec((TOP_K, tm), lambda i: (0, i)),
                   rows(LANES),
                   pl.BlockSpec((TOP_K, tm), lambda i: (0, i)),
                   pl.BlockSpec((N_EXPERTS, LANES), lambda i: (0, 0))),
        scratch_shapes=[pltpu.VMEM((N_EXPERTS, LANES), F32)],
        compiler_params=_cparams(("arbitrary",)),
        name="merge_router",
    )(ya, yb, z, z, z, z, x2d, wa, wb, wo, gain, rw_t, rb_col, upper)


def _dispatch_kernel(pad_end_ref, padded_ref, pos_ref, src_ref, dst_ref, zero_s, zsem, sem):
    td = src_ref.shape[0]

    @pl.when(pl.program_id(0) == 0)
    def _():
        zero_s[...] = jnp.zeros_like(zero_s)

        def tail_copy(e):
            start = pl.multiple_of(pad_end_ref[e] - EXPERT_TILE, EXPERT_TILE)
            return pltpu.make_async_copy(zero_s, dst_ref.at[pl.ds(start, EXPERT_TILE)], zsem)

        for e in range(N_EXPERTS):
            @pl.when(padded_ref[e] > 0)
            def _():
                tail_copy(e).start()
        for e in range(N_EXPERTS):
            @pl.when(padded_ref[e] > 0)
            def _():
                tail_copy(e).wait()

    def issue(r, carry):
        for k in range(TOP_K):
            pltpu.make_async_copy(src_ref.at[pl.ds(r, 1)],
                                  dst_ref.at[pl.ds(pos_ref[k, r], 1)], sem).start()
        return carry

    lax.fori_loop(0, td, issue, 0, unroll=8)
    for k in range(TOP_K):
        pltpu.make_async_copy(src_ref, dst_ref.at[pl.ds(0, td)], sem).wait()


def _dispatch(pad_end, padded, pos, src, n_rows, td):
    t, d = src.shape
    grid_spec = pltpu.PrefetchScalarGridSpec(
        num_scalar_prefetch=2,
        grid=(t // td,),
        in_specs=[pl.BlockSpec((TOP_K, td), lambda i, pe, pd: (0, i), memory_space=pltpu.SMEM),
                  pl.BlockSpec((td, d), lambda i, pe, pd: (i, 0))],
        out_specs=pl.BlockSpec(memory_space=pl.ANY),
        scratch_shapes=[pltpu.VMEM((EXPERT_TILE, d), src.dtype),
                        pltpu.SemaphoreType.DMA, pltpu.SemaphoreType.DMA],
    )
    return pl.pallas_call(
        _dispatch_kernel,
        out_shape=jax.ShapeDtypeStruct((n_rows, d), src.dtype),
        grid_spec=grid_spec,
        compiler_params=pltpu.CompilerParams(dimension_semantics=("arbitrary",),
                                             vmem_limit_bytes=V7X_VMEM_LIMIT_BYTES,
                                             has_side_effects=True),
        name="dispatch",
    )(pad_end, padded, pos, src)


def _expert_kernel(te_ref, tv_ref, xs_ref, wgu_hbm, wdn_hbm, bgu_ref, bdn_ref, o_ref,
                   wgu_s, wdn_s, stage_s, sems, xb_s, act_s, cur_ref):
    i = pl.program_id(0)
    valid = tv_ref[i] > 0
    e = te_ref[i]
    d, two_ff = wgu_s.shape
    d_ff = two_ff // 2
    half = d // 2
    n_slots, rows_per, _ = stage_s.shape
    n_gu = 2 * (d // rows_per)
    n_all = n_gu + d_ff // rows_per

    @pl.when(i == 0)
    def _():
        cur_ref[0] = -1

    @pl.when(jnp.logical_and(valid, cur_ref[0] != e))
    def _():
        def piece(j):
            if j < n_gu:
                return j // 2, j % 2
            return j - n_gu, 0

        def chunk_copy(j):
            r, c = piece(j)
            rows = pl.ds(r * rows_per, rows_per)
            if j < n_gu:
                src = wgu_hbm.at[e, rows, pl.ds(c * d_ff, d_ff)]
            else:
                src = wdn_hbm.at[e, rows, :]
            return pltpu.make_async_copy(src, stage_s.at[j % n_slots], sems.at[j % n_slots])

        for j in range(n_slots - 1):
            chunk_copy(j).start()
        for j in range(n_all):
            if j + n_slots - 1 < n_all:
                chunk_copy(j + n_slots - 1).start()
            chunk_copy(j).wait()
            w = stage_s[j % n_slots].astype(BF16)
            r, c = piece(j)
            if j < n_gu:
                wgu_s[r * rows_per:(r + 1) * rows_per, c * d_ff:(c + 1) * d_ff] = w
            else:
                wdn_s[r * rows_per:(r + 1) * rows_per, :] = w
        cur_ref[0] = e

    @pl.when(valid)
    def _():
        w = xs_ref[...]
        xb_s[:, 0:half] = pltpu.bitcast(w << 16, F32).astype(BF16)
        xb_s[:, half:] = pltpu.bitcast(w & jnp.uint32(0xFFFF0000), F32).astype(BF16)
        xb = xb_s[...]
        for c in range(d_ff // FF_CHUNK):
            ca = slice(c * FF_CHUNK, (c + 1) * FF_CHUNK)
            cu = slice(d_ff + c * FF_CHUNK, d_ff + (c + 1) * FF_CHUNK)
            a = jnp.dot(xb, wgu_s[:, ca], preferred_element_type=F32) + bgu_ref[:, ca]
            u = jnp.dot(xb, wgu_s[:, cu], preferred_element_type=F32) + bgu_ref[:, cu]
            a = jnp.minimum(a, SWIGLU_LIMIT)
            u = jnp.clip(u, -SWIGLU_LIMIT, SWIGLU_LIMIT)
            act_s[:, ca] = ((u + 1.0) * a * _sigmoid(SWIGLU_ALPHA * a)).astype(BF16)
        o_ref[...] = jnp.dot(act_s[...], wdn_s[...], preferred_element_type=F32) + bdn_ref[...]

    @pl.when(jnp.logical_not(valid))
    def _():
        o_ref[...] = jnp.zeros_like(o_ref)


def _experts(tile_expert, tile_valid, xs, w_gu, b_gu, w_dn, b_dn):
    n_rows, half = xs.shape
    d = 2 * half
    d_ff = w_dn.shape[1]
    n_tiles = n_rows // EXPERT_TILE

    grid_spec = pltpu.PrefetchScalarGridSpec(
        num_scalar_prefetch=2,
        grid=(n_tiles,),
        in_specs=[
            pl.BlockSpec((EXPERT_TILE, half), lambda i, te, tv: (jnp.minimum(i, tv[n_tiles]), 0)),
            pl.BlockSpec(memory_space=pl.ANY),
            pl.BlockSpec(memory_space=pl.ANY),
            pl.BlockSpec((None, 1, 2 * d_ff), lambda i, te, tv: (te[i], 0, 0)),
            pl.BlockSpec((None, 1, d), lambda i, te, tv: (te[i], 0, 0)),
        ],
        out_specs=pl.BlockSpec((EXPERT_TILE, d), lambda i, te, tv: (i, 0)),
        scratch_shapes=[pltpu.VMEM((d, 2 * d_ff), BF16),
                        pltpu.VMEM((d_ff, d), BF16),
                        pltpu.VMEM((WEIGHT_RING, WEIGHT_ROWS, d_ff), F32),
                        pltpu.SemaphoreType.DMA((WEIGHT_RING,)),
                        pltpu.VMEM((EXPERT_TILE, d), BF16),
                        pltpu.VMEM((EXPERT_TILE, d_ff), BF16),
                        pltpu.SMEM((1,), I32)],
    )
    assert d == d_ff, "weight staging shares one (WEIGHT_ROWS, d_ff) buffer for w_gu and w_dn pieces"
    return pl.pallas_call(
        _expert_kernel,
        out_shape=jax.ShapeDtypeStruct((n_rows, d), F32),
        grid_spec=grid_spec,
        compiler_params=_cparams(("arbitrary",)),
        name="experts",
    )(tile_expert, tile_valid, xs, w_gu, w_dn, b_gu, b_dn)


def _combine_kernel(pos_ref, posn_ref, ys_ref, x1_ref, gcol_ref, p_ref, g_ple_ref, wpg_ref,
                    wple_ref, g_fin_ref, o_ref, buf, sems):
    tc = x1_ref.shape[0]
    i = pl.program_id(0)
    slot = lax.rem(i, 2)

    def gather(idx_ref, dst_slot):
        def issue(r, carry):
            for k in range(TOP_K):
                pltpu.make_async_copy(ys_ref.at[pl.ds(idx_ref[k, r], 1)],
                                      buf.at[dst_slot, k, pl.ds(r, 1)], sems.at[dst_slot]).start()
            return carry
        lax.fori_loop(0, tc, issue, 0, unroll=True)

    @pl.when(i == 0)
    def _():
        gather(pos_ref, slot)

    gather(posn_ref, 1 - slot)

    def wait_slot(s):
        for k in range(TOP_K):
            pltpu.make_async_copy(ys_ref.at[pl.ds(0, tc)], buf.at[s, k], sems.at[s]).wait()

    ple = jnp.dot(p_ref[...].astype(BF16), wple_ref[...], preferred_element_type=F32)
    wait_slot(slot)

    @pl.when(i + 1 == pl.num_programs(0))
    def _():
        wait_slot(1 - slot)

    gcol = gcol_ref[...]
    moe = buf[slot, 0] * gcol[:, 0:1]
    for k in range(1, TOP_K):
        moe = moe + buf[slot, k] * gcol[:, k:k + 1]
    x2 = x1_ref[...] + moe
    ms = jnp.mean(x2 * x2, axis=-1, keepdims=True)
    h3 = (x2 * lax.rsqrt(ms + EPS) * g_ple_ref[...]).astype(BF16)
    gate = _sigmoid(jnp.dot(h3, wpg_ref[...], preferred_element_type=F32))
    x3 = x2 + ple * gate
    ms3 = jnp.mean(x3 * x3, axis=-1, keepdims=True)
    o_ref[...] = x3 * lax.rsqrt(ms3 + EPS) * g_fin_ref[...]


def _combine(pos, ys, x1, gcol, p2d, g_ple, wpg, wple, g_fin, tc):
    t, d = x1.shape
    n_steps = t // tc

    def rows(width):
        return pl.BlockSpec((tc, width), lambda i: (i, 0))

    def const(shape):
        return pl.BlockSpec(shape, lambda i: (0,) * len(shape), pipeline_mode=pl.Buffered(1))

    return pl.pallas_call(
        _combine_kernel,
        out_shape=jax.ShapeDtypeStruct((t, d), F32),
        grid=(t // tc,),
        in_specs=[pl.BlockSpec((TOP_K, tc), lambda i: (0, i), memory_space=pltpu.SMEM),
                  pl.BlockSpec((TOP_K, tc), lambda i: (0, jnp.minimum(i + 1, n_steps - 1)),
                               memory_space=pltpu.SMEM),
                  pl.BlockSpec(memory_space=pl.ANY),
                  rows(d), rows(LANES), rows(p2d.shape[1]),
                  const((1, d)), const(wpg.shape), const(wple.shape), const((1, d))],
        out_specs=rows(d),
        scratch_shapes=[pltpu.VMEM((2, TOP_K, tc, d), F32), pltpu.SemaphoreType.DMA((2,))],
        compiler_params=_cparams(("arbitrary",)),
        name="combine_ple",
    )(pos, pos, ys, x1, gcol, p2d, g_ple, wpg, wple, g_fin)


def _pick(n, pref):
    return pref if n % pref == 0 else n


def kernel(x, p, norm_mix, w_in, hgrn_lb, hgrn_norm, mlstm_conv_w, mlstm_conv_b, mlstm_b_i,
           mlstm_b_f, mlstm_norm, w_branch_a, w_branch_b, w_out, norm_moe, router_w, router_b,
           exp_w_gu, exp_b_gu, exp_w_dn, exp_b_dn, norm_ple, w_ple, w_ple_gate, norm_final):
    b, s, d = x.shape
    t = b * s
    depth = norm_mix.shape[0]
    wa_cols = H_A * DK_A
    wb_cols = H_B * DV_B
    g0 = 4 * wa_cols + 2 * wb_cols
    g1 = g0 + 2 * H_B
    tril = jnp.tril(jnp.ones((CHUNK, CHUNK), F32)).astype(BF16)
    x2d = x.reshape(t, d)

    for i in range(depth):
        w_i = w_in[i]
        w_main = jnp.concatenate([w_i[:, :g0], w_i[:, g1:]], axis=1).astype(BF16)
        w_gate = jnp.pad(w_i[:, g0:g1], ((0, 0), (0, LANES - 2 * H_B))).astype(BF16)
        gate_bias = jnp.pad(jnp.concatenate([mlstm_b_i[i], mlstm_b_f[i]]),
                            (0, LANES - 2 * H_B)).reshape(1, LANES)

        tm_in = _pick(t, 1024)
        tn_in = _pick(w_main.shape[1], 1408)
        z, gc = _inproj(x2d, norm_mix[i].reshape(1, d), w_main, w_gate, tm_in, tn_in)
        z3 = z.reshape(b, s, z.shape[1])
        gc3 = gc.reshape(b, s, LANES)

        ts = _pick(s, 1024)
        ya, yb = _mixers(z3, gc3, hgrn_lb[i:i + 2], hgrn_norm[i].reshape(1, wa_cols),
                         mlstm_conv_w[i], mlstm_conv_b[i].reshape(1, -1), gate_bias,
                         mlstm_norm[i].reshape(1, wb_cols), tril, ts)

        tm_mg = _pick(t, 512)
        upper = jnp.triu(jnp.ones((tm_mg, tm_mg), F32), 1).astype(BF16)
        x1, hp, idx, gate_rows, gcol, rank, cnt = _merge(
            ya.reshape(t, wa_cols), yb.reshape(t, wb_cols), z, x2d,
            w_branch_a[i].astype(BF16), w_branch_b[i].astype(BF16), w_out[i].astype(BF16),
            norm_moe[i].reshape(1, d), router_w[i].T, router_b[i].reshape(N_EXPERTS, 1),
            upper, tm_mg)
        del gate_rows

        counts = cnt[:, 0].astype(I32)
        padded = (counts + EXPERT_TILE - 1) // EXPERT_TILE * EXPERT_TILE
        pad_end = jnp.cumsum(padded)
        pad_start = pad_end - padded
        e_ids = jnp.arange(N_EXPERTS, dtype=I32)
        pos = rank + jnp.sum(jnp.where(idx[..., None] == e_ids, pad_start, 0), axis=-1)
        n_rows = t * TOP_K + N_EXPERTS * EXPERT_TILE
        n_tiles = n_rows // EXPERT_TILE
        tile_row = jnp.arange(n_tiles, dtype=I32) * EXPERT_TILE
        n_valid = pad_end[-1] // EXPERT_TILE
        last_valid = jnp.maximum(n_valid - 1, 0)
        tile_valid = (tile_row < pad_end[-1]).astype(I32)
        tile_expert = jnp.sum((tile_row[:, None] >= pad_end[None, :]).astype(I32), axis=1)
        tile_expert = jnp.minimum(tile_expert, N_EXPERTS - 1)
        last_e = jnp.sum(jnp.where(jnp.arange(n_tiles) == last_valid, tile_expert, 0))
        tile_expert = jnp.where(tile_valid > 0, tile_expert, last_e).astype(I32)
        tile_valid = jnp.concatenate([tile_valid, last_valid.reshape(1).astype(I32)])

        xs = _dispatch(pad_end.astype(I32), padded, pos, hp, n_rows, _pick(t, 512))
        ys = _experts(tile_expert, tile_valid, xs,
                      exp_w_gu[i], exp_b_gu[i].reshape(N_EXPERTS, 1, -1),
                      exp_w_dn[i], exp_b_dn[i].reshape(N_EXPERTS, 1, d))

        is_last = i == depth - 1
        assert is_last, "the fused final-norm epilogue assumes a single layer"
        x2d = _combine(pos, ys, x1, gcol, p[i].reshape(t, -1), norm_ple[i].reshape(1, d),
                       w_ple_gate[i].astype(BF16), w_ple[i].astype(BF16),
                       norm_final.reshape(1, d), _pick(t, 256))
    return x2d.reshape(b, s, d)
```

```python
import functools

import jax
import jax.numpy as jnp
from jax import lax
from jax.experimental import pallas as pl
from jax.experimental.pallas import tpu as pltpu

F32 = jnp.float32
BF16 = jnp.bfloat16
U32 = jnp.uint32
I32 = jnp.int32

EPS = 1e-6
CHUNK = 64
H_A, DK_A, DV_A = 8, 128, 128
H_B, DQK_B, DV_B = 4, 128, 256
CONV_W = 4
M_INIT = -1e30
N_EXPERTS = 32
TOP_K = 4
SWIGLU_LIMIT = 7.0
SWIGLU_ALPHA = 1.702

V7X_VMEM_LIMIT_BYTES = 56 * 1024 * 1024
LANES = 128

EXPERT_TILE = 512
FF_CHUNK = 512
WEIGHT_ROWS = 256
WEIGHT_RING = 4


def _cparams(sem):
    return pltpu.CompilerParams(dimension_semantics=sem, vmem_limit_bytes=V7X_VMEM_LIMIT_BYTES)


def _sigmoid(x):
    return jax.nn.sigmoid(x)


def _split3(x):
    hi = x.astype(BF16)
    r1 = x - hi.astype(F32)
    mid = r1.astype(BF16)
    lo = (r1 - mid.astype(F32)).astype(BF16)
    return hi, mid, lo


def _cumsum_rows(tril, x):
    hi, mid, lo = _split3(x)
    acc = jnp.dot(tril, hi, preferred_element_type=F32)
    acc = acc + jnp.dot(tril, mid, preferred_element_type=F32)
    return acc + jnp.dot(tril, lo, preferred_element_type=F32)


def _rows_to_tiles(x):
    r, w = x.shape
    n_sub = w // LANES
    parts = [x[:, s * LANES:(s + 1) * LANES].reshape(r // 8, 8, LANES) for s in range(n_sub)]
    v = jnp.stack(parts, axis=1)
    return pltpu.einshape("gstl->gtsl", v).reshape(r, n_sub, LANES)


def _tiles_to_rows(v):
    r, n_sub, _ = v.shape
    v = pltpu.einshape("gtsl->gstl", v.reshape(r // 8, 8, n_sub, LANES))
    return jnp.concatenate([v[:, s].reshape(r, LANES) for s in range(n_sub)], axis=1)


def _dot_tn(a, b):
    return lax.dot_general(a, b, (((0,), (0,)), ((), ())), preferred_element_type=F32)


def _dot_nt(a, b):
    return lax.dot_general(a, b, (((1,), (1,)), ((), ())), preferred_element_type=F32)


def _inproj_kernel(x_ref, g_ref, w_ref, wg_ref, z_ref, gc_ref, h_scr):
    @pl.when(pl.program_id(1) == 0)
    def _():
        x = x_ref[...]
        ms = jnp.mean(x * x, axis=-1, keepdims=True)
        hb = (x * lax.rsqrt(ms + EPS) * g_ref[...]).astype(BF16)
        h_scr[...] = hb
        gc_ref[...] = jnp.dot(hb, wg_ref[...], preferred_element_type=F32)

    z_ref[...] = jnp.dot(h_scr[...], w_ref[...], preferred_element_type=F32).astype(BF16)


def _inproj(x2d, gain, w_main, w_gate, tm, tn):
    t, d = x2d.shape
    n = w_main.shape[1]
    return pl.pallas_call(
        _inproj_kernel,
        out_shape=(jax.ShapeDtypeStruct((t, n), BF16), jax.ShapeDtypeStruct((t, LANES), F32)),
        grid=(t // tm, n // tn),
        in_specs=[
            pl.BlockSpec((tm, d), lambda i, j: (i, 0)),
            pl.BlockSpec((1, d), lambda i, j: (0, 0)),
            pl.BlockSpec((d, tn), lambda i, j: (0, j)),
            pl.BlockSpec((d, LANES), lambda i, j: (0, 0)),
        ],
        out_specs=(
            pl.BlockSpec((tm, tn), lambda i, j: (i, j)),
            pl.BlockSpec((tm, LANES), lambda i, j: (i, 0)),
        ),
        scratch_shapes=[pltpu.VMEM((tm, d), BF16)],
        compiler_params=_cparams(("parallel", "arbitrary")),
        name="inproj",
    )(x2d, gain, w_main, w_gate)


def _hgrn_chunk_fn(q_ref, f_ref, v_ref, g_ref, lb_ref, gain_ref, tril_ref, o_ref, st_ref):
    lbr = lb_ref[...]
    mx = jnp.maximum(lbr[0:1], lbr[1:2])
    e0 = jnp.exp(lbr[0:1] - mx)
    e1 = jnp.exp(lbr[1:2] - mx)
    lb = e0 / (e0 + e1)
    one_m_lb = 1.0 - lb
    tril = tril_ref[...]
    gain = gain_ref[...]
    rows = lax.broadcasted_iota(I32, (CHUNK, CHUNK), 0)
    cols = lax.broadcasted_iota(I32, (CHUNK, CHUNK), 1)
    causal = rows >= cols

    def chunk(c):
        r0 = pl.multiple_of(c * CHUNK, CHUNK)
        q = q_ref[pl.ds(r0, CHUNK), :].astype(F32)
        fl = f_ref[pl.ds(r0, CHUNK), :].astype(F32)
        log_f = jnp.log(lb + one_m_lb * _sigmoid(fl))
        k = one_m_lb * _sigmoid(-fl)
        qs = q * _sigmoid(q)
        bc = _cumsum_rows(tril, log_f)
        b_ref = bc[CHUNK // 2:CHUNK // 2 + 1, :]
        b_end = bc[CHUNK - 1:CHUNK, :]
        qd = (qs * jnp.exp(bc - b_ref)).astype(BF16)
        kd = (k * jnp.exp(b_ref - bc)).astype(BF16)
        ke = (k * jnp.exp(b_end - bc)).astype(BF16)
        qe = (qs * jnp.exp(bc)).astype(BF16)
        decay = jnp.exp(b_end)
        for h in range(H_A):
            sl = slice(h * DK_A, (h + 1) * DK_A)
            scores = _dot_nt(qd[:, sl], kd[:, sl])
            scores = jnp.where(causal, scores, 0.0).astype(BF16)
            vh = v_ref[pl.ds(r0, CHUNK), sl]
            st = st_ref[h]
            o = jnp.dot(scores, vh, preferred_element_type=F32)
            o = o + _dot_nt(qe[:, sl], st.astype(BF16))
            u_t = _dot_tn(vh, ke[:, sl])
            st_ref[h] = st * decay[:, sl] + u_t
            ms = jnp.mean(o * o, axis=-1, keepdims=True)
            gt = g_ref[pl.ds(r0, CHUNK), sl].astype(F32)
            y = o * lax.rsqrt(ms + EPS) * gain[:, sl] * (gt * _sigmoid(gt))
            o_ref[pl.ds(r0, CHUNK), sl] = y.astype(BF16)

    return chunk


def _log_sigmoid(x):
    return jnp.minimum(x, 0.0) - jnp.log1p(jnp.exp(-jnp.abs(x)))


def _mlstm_chunk_fn(qk_ref, v_ref, o_in_ref, gc_ref, cw_ref, cb_ref, gb_ref, gain_ref, tril_ref,
                    o_ref, c_ref, n_ref, m_ref, tail_ref, ext_s):
    tril = tril_ref[...]
    gain = gain_ref[...]
    cw = cw_ref[...]
    cb = cb_ref[...]
    gb = gb_ref[...]
    rows = lax.broadcasted_iota(I32, (CHUNK, CHUNK), 0)
    cols = lax.broadcasted_iota(I32, (CHUNK, CHUNK), 1)
    causal = rows >= cols
    hq = H_B * DQK_B
    scale = DQK_B ** -0.5

    def chunk(c):
        r0 = pl.multiple_of(c * CHUNK, CHUNK)
        u = qk_ref[pl.ds(r0, CHUNK), :].astype(F32)
        ext_s[0:8, :] = tail_ref[...]
        ext_s[8:8 + CHUNK, :] = u
        tail_ref[...] = u[CHUNK - 8:CHUNK, :]
        conv = cb + cw[CONV_W - 1:CONV_W, :] * u
        for j in range(CONV_W - 1):
            lag = CONV_W - 1 - j
            conv = conv + cw[j:j + 1, :] * ext_s[8 - lag:8 - lag + CHUNK, :]
        qk = conv * _sigmoid(conv)
        q = qk[:, :hq]
        k = qk[:, hq:] * scale

        g = gc_ref[pl.ds(r0, CHUNK), :] + gb
        fc = _cumsum_rows(tril, _log_sigmoid(g))
        zpad = jnp.zeros((LANES - CHUNK, LANES), F32)
        g_t = jnp.concatenate([g, zpad], axis=0).T
        fc_t = jnp.concatenate([fc, zpad], axis=0).T

        for h in range(H_B):
            ig_c = g[:, h:h + 1]
            fc_c = fc[:, H_B + h:H_B + h + 1]
            ig_r = g_t[h:h + 1, 0:CHUNK]
            fc_r = fc_t[H_B + h:H_B + h + 1, 0:CHUNK]
            g_tot = fc_c[CHUNK - 1:CHUNK, :]
            qh = q[:, h * DQK_B:(h + 1) * DQK_B]
            kh = k[:, h * DQK_B:(h + 1) * DQK_B]
            vh = v_ref[pl.ds(r0, CHUNK), h * DV_B:(h + 1) * DV_B]
            qb = qh.astype(BF16)
            c_st = c_ref[h]
            n_st = n_ref[h]
            m_st = m_ref[h][0:1, 0:1]

            d_log = jnp.where(causal, fc_c - fc_r + ig_r, -jnp.inf)
            a_int = fc_c + m_st
            m_t = jnp.maximum(jnp.max(d_log, axis=-1, keepdims=True), a_int)
            w_in = jnp.exp(d_log - m_t)
            w_state = jnp.exp(a_int - m_t)
            qkw = _dot_nt(qb, kh.astype(BF16)) * w_in
            num = (jnp.dot(qkw.astype(BF16), vh, preferred_element_type=F32)
                   + w_state * jnp.dot(qb, c_st.astype(BF16), preferred_element_type=F32))
            den = (jnp.sum(qkw, axis=-1, keepdims=True)
                   + w_state * jnp.sum(qh * n_st, axis=-1, keepdims=True))
            hh = num / jnp.maximum(jnp.abs(den), jnp.exp(-m_t))

            a_end = g_tot - fc_c + ig_c
            m_loc = jnp.max(a_end, axis=0, keepdims=True)
            kw = kh * jnp.exp(a_end - m_loc)
            c_loc = _dot_tn(kw.astype(BF16), vh)
            n_loc = jnp.sum(kw, axis=0, keepdims=True)
            m_new = jnp.maximum(g_tot + m_st, m_loc)
            s_prev = jnp.exp(g_tot + m_st - m_new)
            s_loc = jnp.exp(m_loc - m_new)
            c_ref[h] = s_prev * c_st + s_loc * c_loc
            n_ref[h] = s_prev * n_st + s_loc * n_loc
            m_ref[h] = jnp.broadcast_to(m_new, m_ref.shape[1:])

            vs = slice(h * DV_B, (h + 1) * DV_B)
            ms = jnp.mean(hh * hh, axis=-1, keepdims=True)
            og = o_in_ref[pl.ds(r0, CHUNK), vs].astype(F32)
            y = hh * lax.rsqrt(ms + EPS) * gain[:, vs] * _sigmoid(og)
            o_ref[pl.ds(r0, CHUNK), vs] = y.astype(BF16)

    return chunk


def _mixers_kernel(q_ref, f_ref, va_ref, g_ref, lb_ref, gain_a_ref,
                   qk_ref, vb_ref, ob_ref, gc_ref, cw_ref, cb_ref, gb_ref, gain_b_ref, tril_ref,
                   ya_ref, yb_ref, st_ref, c_ref, n_ref, m_ref, tail_ref, ext_s):
    @pl.when(pl.program_id(1) == 0)
    def _():
        st_ref[...] = jnp.zeros_like(st_ref)
        c_ref[...] = jnp.zeros_like(c_ref)
        n_ref[...] = jnp.zeros_like(n_ref)
        m_ref[...] = jnp.full_like(m_ref, M_INIT)
        tail_ref[...] = jnp.zeros_like(tail_ref)

    hgrn_chunk = _hgrn_chunk_fn(q_ref, f_ref, va_ref, g_ref, lb_ref, gain_a_ref, tril_ref,
                                ya_ref, st_ref)
    mlstm_chunk = _mlstm_chunk_fn(qk_ref, vb_ref, ob_ref, gc_ref, cw_ref, cb_ref, gb_ref,
                                  gain_b_ref, tril_ref, yb_ref, c_ref, n_ref, m_ref, tail_ref,
                                  ext_s)

    def body(c, carry):
        hgrn_chunk(c)
        mlstm_chunk(c)
        return carry

    lax.fori_loop(0, q_ref.shape[0] // CHUNK, body, 0)


def _mixers(z3, gc3, lb_raw, gain_a, conv_w, conv_b, gate_bias, gain_b, tril, ts):
    b, s, _ = z3.shape
    w = H_A * DK_A
    assert w == H_B * DV_B == 2 * H_B * DQK_B, "column blocks of z share one width"

    def zspec(col):
        return pl.BlockSpec((None, ts, w), lambda i, j, col=col: (i, j, col))

    def const(shape):
        return pl.BlockSpec(shape, lambda i, j: (0,) * len(shape))

    out = jax.ShapeDtypeStruct((b, s, w), BF16)
    out_spec = pl.BlockSpec((None, ts, w), lambda i, j: (i, j, 0))
    return pl.pallas_call(
        _mixers_kernel,
        out_shape=(out, out),
        grid=(b, s // ts),
        in_specs=[zspec(0), zspec(1), zspec(2), zspec(3), const((2, w)), const((1, w)),
                  zspec(4), zspec(5), zspec(6),
                  pl.BlockSpec((None, ts, LANES), lambda i, j: (i, j, 0)),
                  const((CONV_W, w)), const((1, w)), const((1, LANES)), const((1, w)),
                  const((CHUNK, CHUNK))],
        out_specs=(out_spec, out_spec),
        scratch_shapes=[pltpu.VMEM((H_A, DV_A, DK_A), F32),
                        pltpu.VMEM((H_B, DQK_B, DV_B), F32),
                        pltpu.VMEM((H_B, 1, DQK_B), F32),
                        pltpu.VMEM((H_B, 8, LANES), F32),
                        pltpu.VMEM((8, w), F32),
                        pltpu.VMEM((8 + CHUNK, w), F32)],
        compiler_params=_cparams(("parallel", "arbitrary")),
        name="mixers",
    )(z3, z3, z3, z3, lb_raw, gain_a, z3, z3, z3, gc3, conv_w, conv_b, gate_bias, gain_b, tril)


def _merge_kernel(ya_ref, yb_ref, ra0_ref, ra1_ref, rb0_ref, rb1_ref, x_ref,
                  wa_ref, wb_ref, wo_ref, g_ref, rw_ref, rb_ref, upper_ref,
                  x1_ref, hp_ref, idx_ref, gate_ref, gcol_ref, rank_ref, cnt_ref, carry_ref):
    @pl.when(pl.program_id(0) == 0)
    def _():
        carry_ref[...] = jnp.zeros_like(carry_ref)

    half = wo_ref.shape[0] // 2
    pa = jnp.dot(ya_ref[...], wa_ref[...], preferred_element_type=F32)
    pb = jnp.dot(yb_ref[...], wb_ref[...], preferred_element_type=F32)
    mix0 = (_sigmoid(ra0_ref[...].astype(F32)) * pa[:, :half]
            + _sigmoid(rb0_ref[...].astype(F32)) * pb[:, :half]).astype(BF16)
    mix1 = (_sigmoid(ra1_ref[...].astype(F32)) * pa[:, half:]
            + _sigmoid(rb1_ref[...].astype(F32)) * pb[:, half:]).astype(BF16)
    x1 = (x_ref[...] + jnp.dot(mix0, wo_ref[0:half, :], preferred_element_type=F32)
          + jnp.dot(mix1, wo_ref[half:, :], preferred_element_type=F32))
    x1_ref[...] = x1

    ms = jnp.mean(x1 * x1, axis=-1, keepdims=True)
    h2 = x1 * lax.rsqrt(ms + EPS) * g_ref[...]
    hb = h2.astype(BF16)
    lo = pltpu.bitcast(hb[:, :half].astype(F32), U32) >> 16
    hi = pltpu.bitcast(hb[:, half:].astype(F32), U32) & jnp.uint32(0xFFFF0000)
    hp_ref[...] = _rows_to_tiles(lo | hi)

    h_lo = (h2 - hb.astype(F32)).astype(BF16)
    rw = rw_ref[...]
    rw_hi = rw.astype(BF16)
    rw_lo = (rw - rw_hi.astype(F32)).astype(BF16)
    logits = (_dot_nt(rw_hi, hb) + _dot_nt(rw_hi, h_lo) + _dot_nt(rw_lo, hb)) + rb_ref[...]

    tm = logits.shape[1]
    erow = lax.broadcasted_iota(I32, (N_EXPERTS, tm), 0)
    work = logits
    vals, idxs = [], []
    for _ in range(TOP_K):
        mval = jnp.max(work, axis=0, keepdims=True)
        idx = jnp.min(jnp.where(work == mval, erow, N_EXPERTS), axis=0, keepdims=True)
        work = jnp.where(erow == idx, -jnp.inf, work)
        vals.append(mval)
        idxs.append(idx)
    exps = [jnp.exp(v - vals[0]) for v in vals]
    denom = exps[0] + exps[1] + exps[2] + exps[3]
    gates = [e / denom for e in exps]

    onehots = [(erow == idx).astype(F32) for idx in idxs]
    mask = onehots[0] + onehots[1] + onehots[2] + onehots[3]
    before = jnp.dot(mask.astype(BF16), upper_ref[...], preferred_element_type=F32)
    before = before + carry_ref[:, 0:1]
    carry = carry_ref[...] + jnp.sum(mask, axis=1, keepdims=True)
    carry_ref[...] = carry
    cnt_ref[...] = carry

    idx_ref[...] = jnp.concatenate(idxs, axis=0)
    gate_rows = jnp.concatenate(gates, axis=0)
    gate_ref[...] = gate_rows
    rank_ref[...] = jnp.concatenate(
        [jnp.sum(oh * before, axis=0, keepdims=True) for oh in onehots], axis=0).astype(I32)
    padded = jnp.concatenate([gate_rows, jnp.zeros((LANES - TOP_K, tm), F32)], axis=0)
    gcol_ref[...] = padded.T


def _merge(ya, yb, z, x2d, wa, wb, wo, gain, rw_t, rb_col, upper, tm):
    t, d = x2d.shape
    half = d // 2
    ra_blk = (4 * H_A * DK_A + 3 * H_B * DV_B) // half

    def rows(width, col=0):
        return pl.BlockSpec((tm, width), lambda i, col=col: (i, col))

    def const(shape):
        return pl.BlockSpec(shape, lambda i: (0,) * len(shape), pipeline_mode=pl.Buffered(1))

    out_shape = (
        jax.ShapeDtypeStruct((t, d), F32),
        jax.ShapeDtypeStruct((t, half // LANES, LANES), U32),
        jax.ShapeDtypeStruct((TOP_K, t), I32),
        jax.ShapeDtypeStruct((TOP_K, t), F32),
        jax.ShapeDtypeStruct((t, LANES), F32),
        jax.ShapeDtypeStruct((TOP_K, t), I32),
        jax.ShapeDtypeStruct((N_EXPERTS, LANES), F32),
    )
    return pl.pallas_call(
        _merge_kernel,
        out_shape=out_shape,
        grid=(t // tm,),
        in_specs=[rows(ya.shape[1]), rows(yb.shape[1]),
                  rows(half, ra_blk), rows(half, ra_blk + 1), rows(half, ra_blk + 2),
                  rows(half, ra_blk + 3), rows(d),
                  const(wa.shape), const(wb.shape), const(wo.shape), const((1, d)),
                  const(rw_t.shape), const((N_EXPERTS, 1)), const((tm, tm))],
        out_specs=(rows(d), pl.BlockSpec((tm, half // LANES, LANES), lambda i: (i, 0, 0)),
                   pl.BlockSpec((TOP_K, tm), lambda i: (0, i)),
                   pl.BlockSpec((TOP_K, tm), lambda i: (0, i)),
                   rows(LANES),
                   pl.BlockSpec((TOP_K, tm), lambda i: (0, i)),
                   pl.BlockSpec((N_EXPERTS, LANES), lambda i: (0, 0))),
        scratch_shapes=[pltpu.VMEM((N_EXPERTS, LANES), F32)],
        compiler_params=_cparams(("arbitrary",)),
        name="merge_router",
    )(ya, yb, z, z, z, z, x2d, wa, wb, wo, gain, rw_t, rb_col, upper)


def _dispatch_kernel(pad_end_ref, padded_ref, pos_ref, src_ref, dst_ref, zero_s, zsem, sem):
    td = src_ref.shape[0]

    @pl.when(pl.program_id(0) == 0)
    def _():
        zero_s[...] = jnp.zeros_like(zero_s)

        def tail_copy(e):
            start = pl.multiple_of(pad_end_ref[e] - EXPERT_TILE, EXPERT_TILE)
            return pltpu.make_async_copy(zero_s, dst_ref.at[pl.ds(start, EXPERT_TILE)], zsem)

        for e in range(N_EXPERTS):
            @pl.when(padded_ref[e] > 0)
            def _():
                tail_copy(e).start()
        for e in range(N_EXPERTS):
            @pl.when(padded_ref[e] > 0)
            def _():
                tail_copy(e).wait()

    def issue(r, carry):
        for k in range(TOP_K):
            pltpu.make_async_copy(src_ref.at[pl.ds(r, 1)],
                                  dst_ref.at[pl.ds(pos_ref[k, r], 1)], sem).start()
        return carry

    lax.fori_loop(0, td, issue, 0, unroll=8)
    for k in range(TOP_K):
        pltpu.make_async_copy(src_ref, dst_ref.at[pl.ds(0, td)], sem).wait()


def _dispatch(pad_end, padded, pos, src, n_rows, td):
    t, sub, lanes = src.shape
    grid_spec = pltpu.PrefetchScalarGridSpec(
        num_scalar_prefetch=2,
        grid=(t // td,),
        in_specs=[pl.BlockSpec((TOP_K, td), lambda i, pe, pd: (0, i), memory_space=pltpu.SMEM),
                  pl.BlockSpec((td, sub, lanes), lambda i, pe, pd: (i, 0, 0))],
        out_specs=pl.BlockSpec(memory_space=pl.ANY),
        scratch_shapes=[pltpu.VMEM((EXPERT_TILE, sub, lanes), src.dtype),
                        pltpu.SemaphoreType.DMA, pltpu.SemaphoreType.DMA],
    )
    return pl.pallas_call(
        _dispatch_kernel,
        out_shape=jax.ShapeDtypeStruct((n_rows, sub, lanes), src.dtype),
        grid_spec=grid_spec,
        compiler_params=pltpu.CompilerParams(dimension_semantics=("arbitrary",),
                                             vmem_limit_bytes=V7X_VMEM_LIMIT_BYTES,
                                             has_side_effects=True),
        name="dispatch",
    )(pad_end, padded, pos, src)


def _expert_kernel(te_ref, tv_ref, xs_ref, wgu_hbm, wdn_hbm, bgu_ref, bdn_ref, o_ref,
                   wgu_s, wdn_s, stage_s, sems, xb_s, act_s, cur_ref):
    i = pl.program_id(0)
    valid = tv_ref[i] > 0
    e = te_ref[i]
    d, two_ff = wgu_s.shape
    d_ff = two_ff // 2
    half = d // 2
    n_slots, rows_per, _ = stage_s.shape
    n_gu = 2 * (d // rows_per)
    n_all = n_gu + d_ff // rows_per

    @pl.when(i == 0)
    def _():
        cur_ref[0] = -1

    @pl.when(jnp.logical_and(valid, cur_ref[0] != e))
    def _():
        def piece(j):
            if j < n_gu:
                return j // 2, j % 2
            return j - n_gu, 0

        def chunk_copy(j):
            r, c = piece(j)
            rows = pl.ds(r * rows_per, rows_per)
            if j < n_gu:
                src = wgu_hbm.at[e, rows, pl.ds(c * d_ff, d_ff)]
            else:
                src = wdn_hbm.at[e, rows, :]
            return pltpu.make_async_copy(src, stage_s.at[j % n_slots], sems.at[j % n_slots])

        for j in range(n_slots - 1):
            chunk_copy(j).start()
        for j in range(n_all):
            if j + n_slots - 1 < n_all:
                chunk_copy(j + n_slots - 1).start()
            chunk_copy(j).wait()
            w = stage_s[j % n_slots].astype(BF16)
            r, c = piece(j)
            if j < n_gu:
                wgu_s[r * rows_per:(r + 1) * rows_per, c * d_ff:(c + 1) * d_ff] = w
            else:
                wdn_s[r * rows_per:(r + 1) * rows_per, :] = w
        cur_ref[0] = e

    @pl.when(valid)
    def _():
        w = _tiles_to_rows(xs_ref[...])
        xb_s[:, 0:half] = pltpu.bitcast(w << 16, F32).astype(BF16)
        xb_s[:, half:] = pltpu.bitcast(w & jnp.uint32(0xFFFF0000), F32).astype(BF16)
        xb = xb_s[...]
        for c in range(d_ff // FF_CHUNK):
            ca = slice(c * FF_CHUNK, (c + 1) * FF_CHUNK)
            cu = slice(d_ff + c * FF_CHUNK, d_ff + (c + 1) * FF_CHUNK)
            a = jnp.dot(xb, wgu_s[:, ca], preferred_element_type=F32) + bgu_ref[:, ca]
            u = jnp.dot(xb, wgu_s[:, cu], preferred_element_type=F32) + bgu_ref[:, cu]
            a = jnp.minimum(a, SWIGLU_LIMIT)
            u = jnp.clip(u, -SWIGLU_LIMIT, SWIGLU_LIMIT)
            act_s[:, ca] = ((u + 1.0) * a * _sigmoid(SWIGLU_ALPHA * a)).astype(BF16)
        y = jnp.dot(act_s[...], wdn_s[...], preferred_element_type=F32) + bdn_ref[...]
        o_ref[...] = _rows_to_tiles(y)

    @pl.when(jnp.logical_not(valid))
    def _():
        o_ref[...] = jnp.zeros_like(o_ref)


def _experts(tile_expert, tile_valid, xs, w_gu, b_gu, w_dn, b_dn):
    n_rows, sub, lanes = xs.shape
    d = 2 * sub * lanes
    d_ff = w_dn.shape[1]
    n_tiles = n_rows // EXPERT_TILE

    grid_spec = pltpu.PrefetchScalarGridSpec(
        num_scalar_prefetch=2,
        grid=(n_tiles,),
        in_specs=[
            pl.BlockSpec((EXPERT_TILE, sub, lanes),
                         lambda i, te, tv: (jnp.minimum(i, tv[n_tiles]), 0, 0)),
            pl.BlockSpec(memory_space=pl.ANY),
            pl.BlockSpec(memory_space=pl.ANY),
            pl.BlockSpec((None, 1, 2 * d_ff), lambda i, te, tv: (te[i], 0, 0)),
            pl.BlockSpec((None, 1, d), lambda i, te, tv: (te[i], 0, 0)),
        ],
        out_specs=pl.BlockSpec((EXPERT_TILE, d // LANES, LANES), lambda i, te, tv: (i, 0, 0)),
        scratch_shapes=[pltpu.VMEM((d, 2 * d_ff), BF16),
                        pltpu.VMEM((d_ff, d), BF16),
                        pltpu.VMEM((WEIGHT_RING, WEIGHT_ROWS, d_ff), F32),
                        pltpu.SemaphoreType.DMA((WEIGHT_RING,)),
                        pltpu.VMEM((EXPERT_TILE, d), BF16),
                        pltpu.VMEM((EXPERT_TILE, d_ff), BF16),
                        pltpu.SMEM((1,), I32)],
    )
    assert d == d_ff, "weight staging shares one (WEIGHT_ROWS, d_ff) buffer for w_gu and w_dn pieces"
    return pl.pallas_call(
        _expert_kernel,
        out_shape=jax.ShapeDtypeStruct((n_rows, d // LANES, LANES), F32),
        grid_spec=grid_spec,
        compiler_params=_cparams(("arbitrary",)),
        name="experts",
    )(tile_expert, tile_valid, xs, w_gu, w_dn, b_gu, b_dn)


def _combine_kernel(pos_ref, posn_ref, ys_ref, x1_ref, gcol_ref, p_ref, g_ple_ref, wpg_ref,
                    wple_ref, g_fin_ref, o_ref, buf, sems):
    tc = x1_ref.shape[0]
    i = pl.program_id(0)
    slot = lax.rem(i, 2)

    def gather(idx_ref, dst_slot):
        def issue(r, carry):
            for k in range(TOP_K):
                pltpu.make_async_copy(ys_ref.at[pl.ds(idx_ref[k, r], 1)],
                                      buf.at[dst_slot, k, pl.ds(r, 1)], sems.at[dst_slot]).start()
            return carry
        lax.fori_loop(0, tc, issue, 0, unroll=True)

    @pl.when(i == 0)
    def _():
        gather(pos_ref, slot)

    gather(posn_ref, 1 - slot)

    def wait_slot(s):
        for k in range(TOP_K):
            pltpu.make_async_copy(ys_ref.at[pl.ds(0, tc)], buf.at[s, k], sems.at[s]).wait()

    ple = jnp.dot(p_ref[...].astype(BF16), wple_ref[...], preferred_element_type=F32)
    wait_slot(slot)

    @pl.when(i + 1 == pl.num_programs(0))
    def _():
        wait_slot(1 - slot)

    gcol = gcol_ref[...]
    moe = _tiles_to_rows(buf[slot, 0]) * gcol[:, 0:1]
    for k in range(1, TOP_K):
        moe = moe + _tiles_to_rows(buf[slot, k]) * gcol[:, k:k + 1]
    x2 = x1_ref[...] + moe
    ms = jnp.mean(x2 * x2, axis=-1, keepdims=True)
    h3 = (x2 * lax.rsqrt(ms + EPS) * g_ple_ref[...]).astype(BF16)
    gate = _sigmoid(jnp.dot(h3, wpg_ref[...], preferred_element_type=F32))
    x3 = x2 + ple * gate
    ms3 = jnp.mean(x3 * x3, axis=-1, keepdims=True)
    o_ref[...] = x3 * lax.rsqrt(ms3 + EPS) * g_fin_ref[...]


def _combine(pos, ys, x1, gcol, p2d, g_ple, wpg, wple, g_fin, tc):
    t, d = x1.shape
    n_steps = t // tc

    def rows(width):
        return pl.BlockSpec((tc, width), lambda i: (i, 0))

    def const(shape):
        return pl.BlockSpec(shape, lambda i: (0,) * len(shape), pipeline_mode=pl.Buffered(1))

    return pl.pallas_call(
        _combine_kernel,
        out_shape=jax.ShapeDtypeStruct((t, d), F32),
        grid=(t // tc,),
        in_specs=[pl.BlockSpec((TOP_K, tc), lambda i: (0, i), memory_space=pltpu.SMEM),
                  pl.BlockSpec((TOP_K, tc), lambda i: (0, jnp.minimum(i + 1, n_steps - 1)),
                               memory_space=pltpu.SMEM),
                  pl.BlockSpec(memory_space=pl.ANY),
                  rows(d), rows(LANES), rows(p2d.shape[1]),
                  const((1, d)), const(wpg.shape), const(wple.shape), const((1, d))],
        out_specs=rows(d),
        scratch_shapes=[pltpu.VMEM((2, TOP_K, tc, d // LANES, LANES), F32),
                        pltpu.SemaphoreType.DMA((2,))],
        compiler_params=_cparams(("arbitrary",)),
        name="combine_ple",
    )(pos, pos, ys, x1, gcol, p2d, g_ple, wpg, wple, g_fin)


def _pick(n, pref):
    return pref if n % pref == 0 else n


def kernel(x, p, norm_mix, w_in, hgrn_lb, hgrn_norm, mlstm_conv_w, mlstm_conv_b, mlstm_b_i,
           mlstm_b_f, mlstm_norm, w_branch_a, w_branch_b, w_out, norm_moe, router_w, router_b,
           exp_w_gu, exp_b_gu, exp_w_dn, exp_b_dn, norm_ple, w_ple, w_ple_gate, norm_final):
    b, s, d = x.shape
    t = b * s
    depth = norm_mix.shape[0]
    wa_cols = H_A * DK_A
    wb_cols = H_B * DV_B
    g0 = 4 * wa_cols + 2 * wb_cols
    g1 = g0 + 2 * H_B
    tril = jnp.tril(jnp.ones((CHUNK, CHUNK), F32)).astype(BF16)
    x2d = x.reshape(t, d)

    for i in range(depth):
        w_i = w_in[i]
        w_main = jnp.concatenate([w_i[:, :g0], w_i[:, g1:]], axis=1).astype(BF16)
        w_gate = jnp.pad(w_i[:, g0:g1], ((0, 0), (0, LANES - 2 * H_B))).astype(BF16)
        gate_bias = jnp.pad(jnp.concatenate([mlstm_b_i[i], mlstm_b_f[i]]),
                            (0, LANES - 2 * H_B)).reshape(1, LANES)

        tm_in = _pick(t, 1024)
        tn_in = _pick(w_main.shape[1], 1408)
        z, gc = _inproj(x2d, norm_mix[i].reshape(1, d), w_main, w_gate, tm_in, tn_in)
        z3 = z.reshape(b, s, z.shape[1])
        gc3 = gc.reshape(b, s, LANES)

        ts = _pick(s, 1024)
        ya, yb = _mixers(z3, gc3, hgrn_lb[i:i + 2], hgrn_norm[i].reshape(1, wa_cols),
                         mlstm_conv_w[i], mlstm_conv_b[i].reshape(1, -1), gate_bias,
                         mlstm_norm[i].reshape(1, wb_cols), tril, ts)

        tm_mg = _pick(t, 512)
        upper = jnp.triu(jnp.ones((tm_mg, tm_mg), F32), 1).astype(BF16)
        x1, hp, idx, gate_rows, gcol, rank, cnt = _merge(
            ya.reshape(t, wa_cols), yb.reshape(t, wb_cols), z, x2d,
            w_branch_a[i].astype(BF16), w_branch_b[i].astype(BF16), w_out[i].astype(BF16),
            norm_moe[i].reshape(1, d), router_w[i].T, router_b[i].reshape(N_EXPERTS, 1),
            upper, tm_mg)
        del gate_rows

        counts = cnt[:, 0].astype(I32)
        padded = (counts + EXPERT_TILE - 1) // EXPERT_TILE * EXPERT_TILE
        pad_end = jnp.cumsum(padded)
        pad_start = pad_end - padded
        e_ids = jnp.arange(N_EXPERTS, dtype=I32)
        pos = rank + jnp.sum(jnp.where(idx[..., None] == e_ids, pad_start, 0), axis=-1)
        n_rows = t * TOP_K + N_EXPERTS * EXPERT_TILE
        n_tiles = n_rows // EXPERT_TILE
        tile_row = jnp.arange(n_tiles, dtype=I32) * EXPERT_TILE
        n_valid = pad_end[-1] // EXPERT_TILE
        last_valid = jnp.maximum(n_valid - 1, 0)
        tile_valid = (tile_row < pad_end[-1]).astype(I32)
        tile_expert = jnp.sum((tile_row[:, None] >= pad_end[None, :]).astype(I32), axis=1)
        tile_expert = jnp.minimum(tile_expert, N_EXPERTS - 1)
        last_e = jnp.sum(jnp.where(jnp.arange(n_tiles) == last_valid, tile_expert, 0))
        tile_expert = jnp.where(tile_valid > 0, tile_expert, last_e).astype(I32)
        tile_valid = jnp.concatenate([tile_valid, last_valid.reshape(1).astype(I32)])

        xs = _dispatch(pad_end.astype(I32), padded, pos, hp, n_rows, _pick(t, 512))
        ys = _experts(tile_expert, tile_valid, xs,
                      exp_w_gu[i], exp_b_gu[i].reshape(N_EXPERTS, 1, -1),
                      exp_w_dn[i], exp_b_dn[i].reshape(N_EXPERTS, 1, d))

        is_last = i == depth - 1
        assert is_last, "the fused final-norm epilogue assumes a single layer"
        x2d = _combine(pos, ys, x1, gcol, p[i].reshape(t, -1), norm_ple[i].reshape(1, d),
                       w_ple_gate[i].astype(BF16), w_ple[i].astype(BF16),
                       norm_final.reshape(1, d), _pick(t, 256))
    return x2d.reshape(b, s, d)
```

```python
import jax
import jax.numpy as jnp
from jax import lax
from jax.experimental import pallas as pl
from jax.experimental.pallas import tpu as pltpu

F32 = jnp.float32
BF16 = jnp.bfloat16
U32 = jnp.uint32
I32 = jnp.int32

EPS = 1e-6
CHUNK = 64
H_A, DK_A, DV_A = 8, 128, 128
H_B, DQK_B, DV_B = 4, 128, 256
CONV_W = 4
M_INIT = -1e30
N_EXPERTS = 32
TOP_K = 4
SWIGLU_LIMIT = 7.0
SWIGLU_ALPHA = 1.702

V7X_VMEM_LIMIT_BYTES = 56 * 1024 * 1024
LANES = 128

EXPERT_TILE = 512
FF_CHUNK = 512
WEIGHT_ROWS = 256
WEIGHT_RING = 4


def _cparams(sem):
    return pltpu.CompilerParams(dimension_semantics=sem, vmem_limit_bytes=V7X_VMEM_LIMIT_BYTES)


def _sigmoid(x):
    return jax.nn.sigmoid(x)


def _split3(x):
    hi = x.astype(BF16)
    r1 = x - hi.astype(F32)
    mid = r1.astype(BF16)
    lo = (r1 - mid.astype(F32)).astype(BF16)
    return hi, mid, lo


def _cumsum_rows(tril, x):
    hi, mid, lo = _split3(x)
    acc = jnp.dot(tril, hi, preferred_element_type=F32)
    acc = acc + jnp.dot(tril, mid, preferred_element_type=F32)
    return acc + jnp.dot(tril, lo, preferred_element_type=F32)


def _dot_tn(a, b):
    return lax.dot_general(a, b, (((0,), (0,)), ((), ())), preferred_element_type=F32)


def _dot_nt(a, b):
    return lax.dot_general(a, b, (((1,), (1,)), ((), ())), preferred_element_type=F32)


def _inproj_kernel(x_ref, g_ref, w_ref, wg_ref, z_ref, gc_ref, h_scr):
    @pl.when(pl.program_id(1) == 0)
    def _():
        x = x_ref[...]
        ms = jnp.mean(x * x, axis=-1, keepdims=True)
        hb = (x * lax.rsqrt(ms + EPS) * g_ref[...]).astype(BF16)
        h_scr[...] = hb
        gc_ref[...] = jnp.dot(hb, wg_ref[...], preferred_element_type=F32)

    z_ref[...] = jnp.dot(h_scr[...], w_ref[...], preferred_element_type=F32).astype(BF16)


def _inproj(x2d, gain, w_main, w_gate, tm, tn):
    t, d = x2d.shape
    n = w_main.shape[1]
    return pl.pallas_call(
        _inproj_kernel,
        out_shape=(jax.ShapeDtypeStruct((t, n), BF16), jax.ShapeDtypeStruct((t, LANES), F32)),
        grid=(t // tm, n // tn),
        in_specs=[
            pl.BlockSpec((tm, d), lambda i, j: (i, 0)),
            pl.BlockSpec((1, d), lambda i, j: (0, 0)),
            pl.BlockSpec((d, tn), lambda i, j: (0, j)),
            pl.BlockSpec((d, LANES), lambda i, j: (0, 0)),
        ],
        out_specs=(
            pl.BlockSpec((tm, tn), lambda i, j: (i, j)),
            pl.BlockSpec((tm, LANES), lambda i, j: (i, 0)),
        ),
        scratch_shapes=[pltpu.VMEM((tm, d), BF16)],
        compiler_params=_cparams(("parallel", "arbitrary")),
        name="inproj",
    )(x2d, gain, w_main, w_gate)


def _hgrn_chunk_fn(q_ref, f_ref, v_ref, g_ref, lb_ref, gain_ref, tril_ref, o_ref, st_ref):
    lbr = lb_ref[...]
    mx = jnp.maximum(lbr[0:1], lbr[1:2])
    e0 = jnp.exp(lbr[0:1] - mx)
    e1 = jnp.exp(lbr[1:2] - mx)
    lb = e0 / (e0 + e1)
    one_m_lb = 1.0 - lb
    tril = tril_ref[...]
    gain = gain_ref[...]
    rows = lax.broadcasted_iota(I32, (CHUNK, CHUNK), 0)
    cols = lax.broadcasted_iota(I32, (CHUNK, CHUNK), 1)
    causal = rows >= cols

    def chunk(c):
        r0 = pl.multiple_of(c * CHUNK, CHUNK)
        q = q_ref[pl.ds(r0, CHUNK), :].astype(F32)
        fl = f_ref[pl.ds(r0, CHUNK), :].astype(F32)
        log_f = jnp.log(lb + one_m_lb * _sigmoid(fl))
        k = one_m_lb * _sigmoid(-fl)
        qs = q * _sigmoid(q)
        bc = _cumsum_rows(tril, log_f)
        b_ref = bc[CHUNK // 2:CHUNK // 2 + 1, :]
        b_end = bc[CHUNK - 1:CHUNK, :]
        qd = (qs * jnp.exp(bc - b_ref)).astype(BF16)
        kd = (k * jnp.exp(b_ref - bc)).astype(BF16)
        ke = (k * jnp.exp(b_end - bc)).astype(BF16)
        qe = (qs * jnp.exp(bc)).astype(BF16)
        decay = jnp.exp(b_end)
        for h in range(H_A):
            sl = slice(h * DK_A, (h + 1) * DK_A)
            scores = _dot_nt(qd[:, sl], kd[:, sl])
            scores = jnp.where(causal, scores, 0.0).astype(BF16)
            vh = v_ref[pl.ds(r0, CHUNK), sl]
            st = st_ref[h]
            o = jnp.dot(scores, vh, preferred_element_type=F32)
            o = o + _dot_nt(qe[:, sl], st.astype(BF16))
            u_t = _dot_tn(vh, ke[:, sl])
            st_ref[h] = st * decay[:, sl] + u_t
            ms = jnp.mean(o * o, axis=-1, keepdims=True)
            gt = g_ref[pl.ds(r0, CHUNK), sl].astype(F32)
            y = o * lax.rsqrt(ms + EPS) * gain[:, sl] * (gt * _sigmoid(gt))
            o_ref[pl.ds(r0, CHUNK), sl] = y.astype(BF16)

    return chunk


def _log_sigmoid(x):
    return jnp.minimum(x, 0.0) - jnp.log1p(jnp.exp(-jnp.abs(x)))


def _mlstm_chunk_fn(qk_ref, v_ref, o_in_ref, gc_ref, cw_ref, cb_ref, gb_ref, gain_ref, tril_ref,
                    o_ref, c_ref, n_ref, m_ref, tail_ref, ext_s):
    tril = tril_ref[...]
    gain = gain_ref[...]
    cw = cw_ref[...]
    cb = cb_ref[...]
    gb = gb_ref[...]
    rows = lax.broadcasted_iota(I32, (CHUNK, CHUNK), 0)
    cols = lax.broadcasted_iota(I32, (CHUNK, CHUNK), 1)
    causal = rows >= cols
    hq = H_B * DQK_B
    scale = DQK_B ** -0.5

    def chunk(c):
        r0 = pl.multiple_of(c * CHUNK, CHUNK)
        u = qk_ref[pl.ds(r0, CHUNK), :].astype(F32)
        ext_s[0:8, :] = tail_ref[...]
        ext_s[8:8 + CHUNK, :] = u
        tail_ref[...] = u[CHUNK - 8:CHUNK, :]
        conv = cb + cw[CONV_W - 1:CONV_W, :] * u
        for j in range(CONV_W - 1):
            lag = CONV_W - 1 - j
            conv = conv + cw[j:j + 1, :] * ext_s[8 - lag:8 - lag + CHUNK, :]
        qk = conv * _sigmoid(conv)
        q = qk[:, :hq]
        k = qk[:, hq:] * scale

        g = gc_ref[pl.ds(r0, CHUNK), :] + gb
        fc = _cumsum_rows(tril, _log_sigmoid(g))
        zpad = jnp.zeros((LANES - CHUNK, LANES), F32)
        g_t = jnp.concatenate([g, zpad], axis=0).T
        fc_t = jnp.concatenate([fc, zpad], axis=0).T

        for h in range(H_B):
            ig_c = g[:, h:h + 1]
            fc_c = fc[:, H_B + h:H_B + h + 1]
            ig_r = g_t[h:h + 1, 0:CHUNK]
            fc_r = fc_t[H_B + h:H_B + h + 1, 0:CHUNK]
            g_tot = fc_c[CHUNK - 1:CHUNK, :]
            qh = q[:, h * DQK_B:(h + 1) * DQK_B]
            kh = k[:, h * DQK_B:(h + 1) * DQK_B]
            vh = v_ref[pl.ds(r0, CHUNK), h * DV_B:(h + 1) * DV_B]
            qb = qh.astype(BF16)
            c_st = c_ref[h]
            n_st = n_ref[h]
            m_st = m_ref[h][0:1, 0:1]

            d_log = jnp.where(causal, fc_c - fc_r + ig_r, -jnp.inf)
            a_int = fc_c + m_st
            m_t = jnp.maximum(jnp.max(d_log, axis=-1, keepdims=True), a_int)
            w_in = jnp.exp(d_log - m_t)
            w_state = jnp.exp(a_int - m_t)
            qkw = _dot_nt(qb, kh.astype(BF16)) * w_in
            num = (jnp.dot(qkw.astype(BF16), vh, preferred_element_type=F32)
                   + w_state * jnp.dot(qb, c_st.astype(BF16), preferred_element_type=F32))
            den = (jnp.sum(qkw, axis=-1, keepdims=True)
                   + w_state * jnp.sum(qh * n_st, axis=-1, keepdims=True))
            hh = num / jnp.maximum(jnp.abs(den), jnp.exp(-m_t))

            a_end = g_tot - fc_c + ig_c
            m_loc = jnp.max(a_end, axis=0, keepdims=True)
            kw = kh * jnp.exp(a_end - m_loc)
            c_loc = _dot_tn(kw.astype(BF16), vh)
            n_loc = jnp.sum(kw, axis=0, keepdims=True)
            m_new = jnp.maximum(g_tot + m_st, m_loc)
            s_prev = jnp.exp(g_tot + m_st - m_new)
            s_loc = jnp.exp(m_loc - m_new)
            c_ref[h] = s_prev * c_st + s_loc * c_loc
            n_ref[h] = s_prev * n_st + s_loc * n_loc
            m_ref[h] = jnp.broadcast_to(m_new, m_ref.shape[1:])

            vs = slice(h * DV_B, (h + 1) * DV_B)
            ms = jnp.mean(hh * hh, axis=-1, keepdims=True)
            og = o_in_ref[pl.ds(r0, CHUNK), vs].astype(F32)
            y = hh * lax.rsqrt(ms + EPS) * gain[:, vs] * _sigmoid(og)
            o_ref[pl.ds(r0, CHUNK), vs] = y.astype(BF16)

    return chunk


def _mixers_kernel(q_ref, f_ref, va_ref, g_ref, lb_ref, gain_a_ref,
                   qk_ref, vb_ref, ob_ref, gc_ref, cw_ref, cb_ref, gb_ref, gain_b_ref, tril_ref,
                   ya_ref, yb_ref, st_ref, c_ref, n_ref, m_ref, tail_ref, ext_s):
    @pl.when(pl.program_id(1) == 0)
    def _():
        st_ref[...] = jnp.zeros_like(st_ref)
        c_ref[...] = jnp.zeros_like(c_ref)
        n_ref[...] = jnp.zeros_like(n_ref)
        m_ref[...] = jnp.full_like(m_ref, M_INIT)
        tail_ref[...] = jnp.zeros_like(tail_ref)

    hgrn_chunk = _hgrn_chunk_fn(q_ref, f_ref, va_ref, g_ref, lb_ref, gain_a_ref, tril_ref,
                                ya_ref, st_ref)
    mlstm_chunk = _mlstm_chunk_fn(qk_ref, vb_ref, ob_ref, gc_ref, cw_ref, cb_ref, gb_ref,
                                  gain_b_ref, tril_ref, yb_ref, c_ref, n_ref, m_ref, tail_ref,
                                  ext_s)

    def body(c, carry):
        hgrn_chunk(c)
        mlstm_chunk(c)
        return carry

    lax.fori_loop(0, q_ref.shape[0] // CHUNK, body, 0)


def _mixers(z3, gc3, lb_raw, gain_a, conv_w, conv_b, gate_bias, gain_b, tril, ts):
    b, s, _ = z3.shape
    w = H_A * DK_A
    assert w == H_B * DV_B == 2 * H_B * DQK_B, "column blocks of z share one width"

    def zspec(col):
        return pl.BlockSpec((None, ts, w), lambda i, j, col=col: (i, j, col))

    def const(shape):
        return pl.BlockSpec(shape, lambda i, j: (0,) * len(shape))

    out = jax.ShapeDtypeStruct((b, s, w), BF16)
    out_spec = pl.BlockSpec((None, ts, w), lambda i, j: (i, j, 0))
    return pl.pallas_call(
        _mixers_kernel,
        out_shape=(out, out),
        grid=(b, s // ts),
        in_specs=[zspec(0), zspec(1), zspec(2), zspec(3), const((2, w)), const((1, w)),
                  zspec(4), zspec(5), zspec(6),
                  pl.BlockSpec((None, ts, LANES), lambda i, j: (i, j, 0)),
                  const((CONV_W, w)), const((1, w)), const((1, LANES)), const((1, w)),
                  const((CHUNK, CHUNK))],
        out_specs=(out_spec, out_spec),
        scratch_shapes=[pltpu.VMEM((H_A, DV_A, DK_A), F32),
                        pltpu.VMEM((H_B, DQK_B, DV_B), F32),
                        pltpu.VMEM((H_B, 1, DQK_B), F32),
                        pltpu.VMEM((H_B, 8, LANES), F32),
                        pltpu.VMEM((8, w), F32),
                        pltpu.VMEM((8 + CHUNK, w), F32)],
        compiler_params=_cparams(("parallel", "arbitrary")),
        name="mixers",
    )(z3, z3, z3, z3, lb_raw, gain_a, z3, z3, z3, gc3, conv_w, conv_b, gate_bias, gain_b, tril)


def _merge_kernel(ya_ref, yb_ref, ra0_ref, ra1_ref, rb0_ref, rb1_ref, x_ref,
                  wa_ref, wb_ref, wo_ref, g_ref, rw_ref, rb_ref, upper_ref,
                  x1_ref, hp_ref, idx_ref, gate_ref, gcol_ref, rank_ref, cnt_ref, carry_ref):
    @pl.when(pl.program_id(0) == 0)
    def _():
        carry_ref[...] = jnp.zeros_like(carry_ref)

    half = wo_ref.shape[0] // 2
    pa = jnp.dot(ya_ref[...], wa_ref[...], preferred_element_type=F32)
    pb = jnp.dot(yb_ref[...], wb_ref[...], preferred_element_type=F32)
    mix0 = (_sigmoid(ra0_ref[...].astype(F32)) * pa[:, :half]
            + _sigmoid(rb0_ref[...].astype(F32)) * pb[:, :half]).astype(BF16)
    mix1 = (_sigmoid(ra1_ref[...].astype(F32)) * pa[:, half:]
            + _sigmoid(rb1_ref[...].astype(F32)) * pb[:, half:]).astype(BF16)
    x1 = (x_ref[...] + jnp.dot(mix0, wo_ref[0:half, :], preferred_element_type=F32)
          + jnp.dot(mix1, wo_ref[half:, :], preferred_element_type=F32))
    x1_ref[...] = x1

    ms = jnp.mean(x1 * x1, axis=-1, keepdims=True)
    h2 = x1 * lax.rsqrt(ms + EPS) * g_ref[...]
    hb = h2.astype(BF16)
    lo = pltpu.bitcast(hb[:, :half].astype(F32), U32) >> 16
    hi = pltpu.bitcast(hb[:, half:].astype(F32), U32) & jnp.uint32(0xFFFF0000)
    hp_ref[...] = lo | hi

    h_lo = (h2 - hb.astype(F32)).astype(BF16)
    rw = rw_ref[...]
    rw_hi = rw.astype(BF16)
    rw_lo = (rw - rw_hi.astype(F32)).astype(BF16)
    logits = (_dot_nt(rw_hi, hb) + _dot_nt(rw_hi, h_lo) + _dot_nt(rw_lo, hb)) + rb_ref[...]

    tm = logits.shape[1]
    erow = lax.broadcasted_iota(I32, (N_EXPERTS, tm), 0)
    work = logits
    vals, idxs = [], []
    for _ in range(TOP_K):
        mval = jnp.max(work, axis=0, keepdims=True)
        idx = jnp.min(jnp.where(work == mval, erow, N_EXPERTS), axis=0, keepdims=True)
        work = jnp.where(erow == idx, -jnp.inf, work)
        vals.append(mval)
        idxs.append(idx)
    exps = [jnp.exp(v - vals[0]) for v in vals]
    denom = exps[0] + exps[1] + exps[2] + exps[3]
    gates = [e / denom for e in exps]

    onehots = [(erow == idx).astype(F32) for idx in idxs]
    mask = onehots[0] + onehots[1] + onehots[2] + onehots[3]
    before = jnp.dot(mask.astype(BF16), upper_ref[...], preferred_element_type=F32)
    before = before + carry_ref[:, 0:1]
    carry = carry_ref[...] + jnp.sum(mask, axis=1, keepdims=True)
    carry_ref[...] = carry
    cnt_ref[...] = carry

    idx_ref[...] = jnp.concatenate(idxs, axis=0)
    gate_rows = jnp.concatenate(gates, axis=0)
    gate_ref[...] = gate_rows
    rank_ref[...] = jnp.concatenate(
        [jnp.sum(oh * before, axis=0, keepdims=True) for oh in onehots], axis=0).astype(I32)
    padded = jnp.concatenate([gate_rows, jnp.zeros((LANES - TOP_K, tm), F32)], axis=0)
    gcol_ref[...] = padded.T


def _merge(ya, yb, z, x2d, wa, wb, wo, gain, rw_t, rb_col, upper, tm):
    t, d = x2d.shape
    half = d // 2
    ra_blk = (4 * H_A * DK_A + 3 * H_B * DV_B) // half

    def rows(width, col=0):
        return pl.BlockSpec((tm, width), lambda i, col=col: (i, col))

    def const(shape):
        return pl.BlockSpec(shape, lambda i: (0,) * len(shape), pipeline_mode=pl.Buffered(1))

    out_shape = (
        jax.ShapeDtypeStruct((t, d), F32),
        jax.ShapeDtypeStruct((t, half), U32),
        jax.ShapeDtypeStruct((TOP_K, t), I32),
        jax.ShapeDtypeStruct((TOP_K, t), F32),
        jax.ShapeDtypeStruct((t, LANES), F32),
        jax.ShapeDtypeStruct((TOP_K, t), I32),
        jax.ShapeDtypeStruct((N_EXPERTS, LANES), F32),
    )
    return pl.pallas_call(
        _merge_kernel,
        out_shape=out_shape,
        grid=(t // tm,),
        in_specs=[rows(ya.shape[1]), rows(yb.shape[1]),
                  rows(half, ra_blk), rows(half, ra_blk + 1), rows(half, ra_blk + 2),
                  rows(half, ra_blk + 3), rows(d),
                  const(wa.shape), const(wb.shape), const(wo.shape), const((1, d)),
                  const(rw_t.shape), const((N_EXPERTS, 1)), const((tm, tm))],
        out_specs=(rows(d), rows(half),
                   pl.BlockSpec((TOP_K, tm), lambda i: (0, i)),
                   pl.BlockSpec((TOP_K, tm), lambda i: (0, i)),
                   rows(LANES),
                   pl.BlockSpec((TOP_K, tm), lambda i: (0, i)),
                   pl.BlockSpec((N_EXPERTS, LANES), lambda i: (0, 0))),
        scratch_shapes=[pltpu.VMEM((N_EXPERTS, LANES), F32)],
        compiler_params=_cparams(("arbitrary",)),
        name="merge_router",
    )(ya, yb, z, z, z, z, x2d, wa, wb, wo, gain, rw_t, rb_col, upper)


def _dispatch_kernel(pad_end_ref, padded_ref, pos_ref, src_ref, dst_ref, zero_s, zsem, sem):
    td = src_ref.shape[0]

    @pl.when(pl.program_id(0) == 0)
    def _():
        zero_s[...] = jnp.zeros_like(zero_s)

        def tail_copy(e):
            start = pl.multiple_of(pad_end_ref[e] - EXPERT_TILE, EXPERT_TILE)
            return pltpu.make_async_copy(zero_s, dst_ref.at[pl.ds(start, EXPERT_TILE)], zsem)

        for e in range(N_EXPERTS):
            @pl.when(padded_ref[e] > 0)
            def _():
                tail_copy(e).start()
        for e in range(N_EXPERTS):
            @pl.when(padded_ref[e] > 0)
            def _():
                tail_copy(e).wait()

    def issue(r, carry):
        for k in range(TOP_K):
            pltpu.make_async_copy(src_ref.at[pl.ds(r, 1)],
                                  dst_ref.at[pl.ds(pos_ref[k, r], 1)], sem).start(priority=k % 2)
        return carry

    lax.fori_loop(0, td, issue, 0, unroll=8)
    for k in range(TOP_K):
        pltpu.make_async_copy(src_ref, dst_ref.at[pl.ds(0, td)], sem).wait()


def _dispatch(pad_end, padded, pos, src, n_rows, td):
    t, d = src.shape
    grid_spec = pltpu.PrefetchScalarGridSpec(
        num_scalar_prefetch=2,
        grid=(t // td,),
        in_specs=[pl.BlockSpec((TOP_K, td), lambda i, pe, pd: (0, i), memory_space=pltpu.SMEM),
                  pl.BlockSpec((td, d), lambda i, pe, pd: (i, 0))],
        out_specs=pl.BlockSpec(memory_space=pl.ANY),
        scratch_shapes=[pltpu.VMEM((EXPERT_TILE, d), src.dtype),
                        pltpu.SemaphoreType.DMA, pltpu.SemaphoreType.DMA],
    )
    return pl.pallas_call(
        _dispatch_kernel,
        out_shape=jax.ShapeDtypeStruct((n_rows, d), src.dtype),
        grid_spec=grid_spec,
        compiler_params=pltpu.CompilerParams(dimension_semantics=("arbitrary",),
                                             vmem_limit_bytes=V7X_VMEM_LIMIT_BYTES,
                                             has_side_effects=True),
        name="dispatch",
    )(pad_end, padded, pos, src)


def _expert_kernel(te_ref, tv_ref, xs_ref, wgu_hbm, wdn_hbm, bgu_ref, bdn_ref, o_ref,
                   wgu_s, wdn_s, stage_s, sems, xb_s, act_s, cur_ref):
    i = pl.program_id(0)
    valid = tv_ref[i] > 0
    e = te_ref[i]
    d, two_ff = wgu_s.shape
    d_ff = two_ff // 2
    half = d // 2
    n_slots, rows_per, _ = stage_s.shape
    n_gu = 2 * (d // rows_per)
    n_all = n_gu + d_ff // rows_per

    @pl.when(i == 0)
    def _():
        cur_ref[0] = -1

    @pl.when(jnp.logical_and(valid, cur_ref[0] != e))
    def _():
        def piece(j):
            if j < n_gu:
                return j // 2, j % 2
            return j - n_gu, 0

        def chunk_copy(j):
            r, c = piece(j)
            rows = pl.ds(r * rows_per, rows_per)
            if j < n_gu:
                src = wgu_hbm.at[e, rows, pl.ds(c * d_ff, d_ff)]
            else:
                src = wdn_hbm.at[e, rows, :]
            return pltpu.make_async_copy(src, stage_s.at[j % n_slots], sems.at[j % n_slots])

        for j in range(n_slots - 1):
            chunk_copy(j).start()
        for j in range(n_all):
            if j + n_slots - 1 < n_all:
                chunk_copy(j + n_slots - 1).start()
            chunk_copy(j).wait()
            w = stage_s[j % n_slots].astype(BF16)
            r, c = piece(j)
            if j < n_gu:
                wgu_s[r * rows_per:(r + 1) * rows_per, c * d_ff:(c + 1) * d_ff] = w
            else:
                wdn_s[r * rows_per:(r + 1) * rows_per, :] = w
        cur_ref[0] = e

    @pl.when(valid)
    def _():
        w = xs_ref[...]
        xb_s[:, 0:half] = pltpu.bitcast(w << 16, F32).astype(BF16)
        xb_s[:, half:] = pltpu.bitcast(w & jnp.uint32(0xFFFF0000), F32).astype(BF16)
        xb = xb_s[...]
        for c in range(d_ff // FF_CHUNK):
            ca = slice(c * FF_CHUNK, (c + 1) * FF_CHUNK)
            cu = slice(d_ff + c * FF_CHUNK, d_ff + (c + 1) * FF_CHUNK)
            a = jnp.dot(xb, wgu_s[:, ca], preferred_element_type=F32) + bgu_ref[:, ca]
            u = jnp.dot(xb, wgu_s[:, cu], preferred_element_type=F32) + bgu_ref[:, cu]
            a = jnp.minimum(a, SWIGLU_LIMIT)
            u = jnp.clip(u, -SWIGLU_LIMIT, SWIGLU_LIMIT)
            act_s[:, ca] = ((u + 1.0) * a * _sigmoid(SWIGLU_ALPHA * a)).astype(BF16)
        o_ref[...] = jnp.dot(act_s[...], wdn_s[...], preferred_element_type=F32) + bdn_ref[...]

    @pl.when(jnp.logical_not(valid))
    def _():
        o_ref[...] = jnp.zeros_like(o_ref)


def _experts(tile_expert, tile_valid, xs, w_gu, b_gu, w_dn, b_dn):
    n_rows, half = xs.shape
    d = 2 * half
    d_ff = w_dn.shape[1]
    n_tiles = n_rows // EXPERT_TILE

    grid_spec = pltpu.PrefetchScalarGridSpec(
        num_scalar_prefetch=2,
        grid=(n_tiles,),
        in_specs=[
            pl.BlockSpec((EXPERT_TILE, half), lambda i, te, tv: (jnp.minimum(i, tv[n_tiles]), 0)),
            pl.BlockSpec(memory_space=pl.ANY),
            pl.BlockSpec(memory_space=pl.ANY),
            pl.BlockSpec((None, 1, 2 * d_ff), lambda i, te, tv: (te[i], 0, 0)),
            pl.BlockSpec((None, 1, d), lambda i, te, tv: (te[i], 0, 0)),
        ],
        out_specs=pl.BlockSpec((EXPERT_TILE, d), lambda i, te, tv: (i, 0)),
        scratch_shapes=[pltpu.VMEM((d, 2 * d_ff), BF16),
                        pltpu.VMEM((d_ff, d), BF16),
                        pltpu.VMEM((WEIGHT_RING, WEIGHT_ROWS, d_ff), F32),
                        pltpu.SemaphoreType.DMA((WEIGHT_RING,)),
                        pltpu.VMEM((EXPERT_TILE, d), BF16),
                        pltpu.VMEM((EXPERT_TILE, d_ff), BF16),
                        pltpu.SMEM((1,), I32)],
    )
    assert d == d_ff, "weight staging shares one (WEIGHT_ROWS, d_ff) buffer for w_gu and w_dn pieces"
    return pl.pallas_call(
        _expert_kernel,
        out_shape=jax.ShapeDtypeStruct((n_rows, d), F32),
        grid_spec=grid_spec,
        compiler_params=_cparams(("arbitrary",)),
        name="experts",
    )(tile_expert, tile_valid, xs, w_gu, w_dn, b_gu, b_dn)


def _combine_kernel(pos_ref, posn_ref, ys_ref, x1_ref, gcol_ref, p_ref, g_ple_ref, wpg_ref,
                    wple_ref, g_fin_ref, o_ref, buf, sems):
    tc = x1_ref.shape[0]
    i = pl.program_id(0)
    slot = lax.rem(i, 2)

    def gather(idx_ref, dst_slot):
        def issue(r, carry):
            for k in range(TOP_K):
                pltpu.make_async_copy(ys_ref.at[pl.ds(idx_ref[k, r], 1)],
                                      buf.at[dst_slot, k, pl.ds(r, 1)],
                                      sems.at[dst_slot]).start(priority=k % 2)
            return carry
        lax.fori_loop(0, tc, issue, 0, unroll=True)

    @pl.when(i == 0)
    def _():
        gather(pos_ref, slot)

    gather(posn_ref, 1 - slot)

    def wait_slot(s):
        for k in range(TOP_K):
            pltpu.make_async_copy(ys_ref.at[pl.ds(0, tc)], buf.at[s, k], sems.at[s]).wait()

    ple = jnp.dot(p_ref[...].astype(BF16), wple_ref[...], preferred_element_type=F32)
    wait_slot(slot)

    @pl.when(i + 1 == pl.num_programs(0))
    def _():
        wait_slot(1 - slot)

    gcol = gcol_ref[...]
    moe = buf[slot, 0] * gcol[:, 0:1]
    for k in range(1, TOP_K):
        moe = moe + buf[slot, k] * gcol[:, k:k + 1]
    x2 = x1_ref[...] + moe
    ms = jnp.mean(x2 * x2, axis=-1, keepdims=True)
    h3 = (x2 * lax.rsqrt(ms + EPS) * g_ple_ref[...]).astype(BF16)
    gate = _sigmoid(jnp.dot(h3, wpg_ref[...], preferred_element_type=F32))
    x3 = x2 + ple * gate
    ms3 = jnp.mean(x3 * x3, axis=-1, keepdims=True)
    o_ref[...] = x3 * lax.rsqrt(ms3 + EPS) * g_fin_ref[...]


def _combine(pos, ys, x1, gcol, p2d, g_ple, wpg, wple, g_fin, tc):
    t, d = x1.shape
    n_steps = t // tc

    def rows(width):
        return pl.BlockSpec((tc, width), lambda i: (i, 0))

    def const(shape):
        return pl.BlockSpec(shape, lambda i: (0,) * len(shape), pipeline_mode=pl.Buffered(1))

    return pl.pallas_call(
        _combine_kernel,
        out_shape=jax.ShapeDtypeStruct((t, d), F32),
        grid=(t // tc,),
        in_specs=[pl.BlockSpec((TOP_K, tc), lambda i: (0, i), memory_space=pltpu.SMEM),
                  pl.BlockSpec((TOP_K, tc), lambda i: (0, jnp.minimum(i + 1, n_steps - 1)),
                               memory_space=pltpu.SMEM),
                  pl.BlockSpec(memory_space=pl.ANY),
                  rows(d), rows(LANES), rows(p2d.shape[1]),
                  const((1, d)), const(wpg.shape), const(wple.shape), const((1, d))],
        out_specs=rows(d),
        scratch_shapes=[pltpu.VMEM((2, TOP_K, tc, d), F32), pltpu.SemaphoreType.DMA((2,))],
        compiler_params=_cparams(("arbitrary",)),
        name="combine_ple",
    )(pos, pos, ys, x1, gcol, p2d, g_ple, wpg, wple, g_fin)


def _pick(n, pref):
    return pref if n % pref == 0 else n


def kernel(x, p, norm_mix, w_in, hgrn_lb, hgrn_norm, mlstm_conv_w, mlstm_conv_b, mlstm_b_i,
           mlstm_b_f, mlstm_norm, w_branch_a, w_branch_b, w_out, norm_moe, router_w, router_b,
           exp_w_gu, exp_b_gu, exp_w_dn, exp_b_dn, norm_ple, w_ple, w_ple_gate, norm_final):
    b, s, d = x.shape
    t = b * s
    depth = norm_mix.shape[0]
    wa_cols = H_A * DK_A
    wb_cols = H_B * DV_B
    g0 = 4 * wa_cols + 2 * wb_cols
    g1 = g0 + 2 * H_B
    tril = jnp.tril(jnp.ones((CHUNK, CHUNK), F32)).astype(BF16)
    x2d = x.reshape(t, d)

    for i in range(depth):
        w_i = w_in[i]
        w_main = jnp.concatenate([w_i[:, :g0], w_i[:, g1:]], axis=1).astype(BF16)
        w_gate = jnp.pad(w_i[:, g0:g1], ((0, 0), (0, LANES - 2 * H_B))).astype(BF16)
        gate_bias = jnp.pad(jnp.concatenate([mlstm_b_i[i], mlstm_b_f[i]]),
                            (0, LANES - 2 * H_B)).reshape(1, LANES)

        tm_in = _pick(t, 1024)
        tn_in = _pick(w_main.shape[1], 1408)
        z, gc = _inproj(x2d, norm_mix[i].reshape(1, d), w_main, w_gate, tm_in, tn_in)
        z3 = z.reshape(b, s, z.shape[1])
        gc3 = gc.reshape(b, s, LANES)

        ts = _pick(s, 1024)
        ya, yb = _mixers(z3, gc3, hgrn_lb[i:i + 2], hgrn_norm[i].reshape(1, wa_cols),
                         mlstm_conv_w[i], mlstm_conv_b[i].reshape(1, -1), gate_bias,
                         mlstm_norm[i].reshape(1, wb_cols), tril, ts)

        tm_mg = _pick(t, 512)
        upper = jnp.triu(jnp.ones((tm_mg, tm_mg), F32), 1).astype(BF16)
        x1, hp, idx, gate_rows, gcol, rank, cnt = _merge(
            ya.reshape(t, wa_cols), yb.reshape(t, wb_cols), z, x2d,
            w_branch_a[i].astype(BF16), w_branch_b[i].astype(BF16), w_out[i].astype(BF16),
            norm_moe[i].reshape(1, d), router_w[i].T, router_b[i].reshape(N_EXPERTS, 1),
            upper, tm_mg)
        del gate_rows

        counts = cnt[:, 0].astype(I32)
        padded = (counts + EXPERT_TILE - 1) // EXPERT_TILE * EXPERT_TILE
        pad_end = jnp.cumsum(padded)
        pad_start = pad_end - padded
        e_ids = jnp.arange(N_EXPERTS, dtype=I32)
        pos = rank + jnp.sum(jnp.where(idx[..., None] == e_ids, pad_start, 0), axis=-1)
        n_rows = t * TOP_K + N_EXPERTS * EXPERT_TILE
        n_tiles = n_rows // EXPERT_TILE
        tile_row = jnp.arange(n_tiles, dtype=I32) * EXPERT_TILE
        n_valid = pad_end[-1] // EXPERT_TILE
        last_valid = jnp.maximum(n_valid - 1, 0)
        tile_valid = (tile_row < pad_end[-1]).astype(I32)
        tile_expert = jnp.sum((tile_row[:, None] >= pad_end[None, :]).astype(I32), axis=1)
        tile_expert = jnp.minimum(tile_expert, N_EXPERTS - 1)
        last_e = jnp.sum(jnp.where(jnp.arange(n_tiles) == last_valid, tile_expert, 0))
        tile_expert = jnp.where(tile_valid > 0, tile_expert, last_e).astype(I32)
        tile_valid = jnp.concatenate([tile_valid, last_valid.reshape(1).astype(I32)])

        xs = _dispatch(pad_end.astype(I32), padded, pos, hp, n_rows, _pick(t, 512))
        ys = _experts(tile_expert, tile_valid, xs,
                      exp_w_gu[i], exp_b_gu[i].reshape(N_EXPERTS, 1, -1),
                      exp_w_dn[i], exp_b_dn[i].reshape(N_EXPERTS, 1, d))

        is_last = i == depth - 1
        assert is_last, "the fused final-norm epilogue assumes a single layer"
        x2d = _combine(pos, ys, x1, gcol, p[i].reshape(t, -1), norm_ple[i].reshape(1, d),
                       w_ple_gate[i].astype(BF16), w_ple[i].astype(BF16),
                       norm_final.reshape(1, d), _pick(t, 256))
    return x2d.reshape(b, s, d)
```

```python
import jax
import jax.numpy as jnp
from jax import lax
from jax.experimental import pallas as pl
from jax.experimental.pallas import tpu as pltpu

F32 = jnp.float32
BF16 = jnp.bfloat16
U32 = jnp.uint32
I32 = jnp.int32

EPS = 1e-6
CHUNK = 64
H_A, DK_A, DV_A = 8, 128, 128
H_B, DQK_B, DV_B = 4, 128, 256
CONV_W = 4
M_INIT = -1e30
N_EXPERTS = 32
TOP_K = 4
SWIGLU_LIMIT = 7.0
SWIGLU_ALPHA = 1.702

V7X_VMEM_LIMIT_BYTES = 56 * 1024 * 1024
LANES = 128

EXPERT_TILE = 512
FF_CHUNK = 512
WEIGHT_ROWS = 256
WEIGHT_RING = 6


def _cparams(sem):
    return pltpu.CompilerParams(dimension_semantics=sem, vmem_limit_bytes=V7X_VMEM_LIMIT_BYTES)


def _sigmoid(x):
    return jax.nn.sigmoid(x)


def _split3(x):
    hi = x.astype(BF16)
    r1 = x - hi.astype(F32)
    mid = r1.astype(BF16)
    lo = (r1 - mid.astype(F32)).astype(BF16)
    return hi, mid, lo


def _cumsum_rows(tril, x):
    hi, mid, lo = _split3(x)
    acc = jnp.dot(tril, hi, preferred_element_type=F32)
    acc = acc + jnp.dot(tril, mid, preferred_element_type=F32)
    return acc + jnp.dot(tril, lo, preferred_element_type=F32)


def _dot_tn(a, b):
    return lax.dot_general(a, b, (((0,), (0,)), ((), ())), preferred_element_type=F32)


def _dot_nt(a, b):
    return lax.dot_general(a, b, (((1,), (1,)), ((), ())), preferred_element_type=F32)


def _inproj_kernel(x_ref, g_ref, w_ref, wg_ref, z_ref, gc_ref, h_scr):
    @pl.when(pl.program_id(1) == 0)
    def _():
        x = x_ref[...]
        ms = jnp.mean(x * x, axis=-1, keepdims=True)
        hb = (x * lax.rsqrt(ms + EPS) * g_ref[...]).astype(BF16)
        h_scr[...] = hb
        gc_ref[...] = jnp.dot(hb, wg_ref[...], preferred_element_type=F32)

    z_ref[...] = jnp.dot(h_scr[...], w_ref[...], preferred_element_type=F32).astype(BF16)


def _inproj(x2d, gain, w_main, w_gate, tm, tn):
    t, d = x2d.shape
    n = w_main.shape[1]
    return pl.pallas_call(
        _inproj_kernel,
        out_shape=(jax.ShapeDtypeStruct((t, n), BF16), jax.ShapeDtypeStruct((t, LANES), F32)),
        grid=(t // tm, n // tn),
        in_specs=[
            pl.BlockSpec((tm, d), lambda i, j: (i, 0)),
            pl.BlockSpec((1, d), lambda i, j: (0, 0)),
            pl.BlockSpec((d, tn), lambda i, j: (0, j)),
            pl.BlockSpec((d, LANES), lambda i, j: (0, 0)),
        ],
        out_specs=(
            pl.BlockSpec((tm, tn), lambda i, j: (i, j)),
            pl.BlockSpec((tm, LANES), lambda i, j: (i, 0)),
        ),
        scratch_shapes=[pltpu.VMEM((tm, d), BF16)],
        compiler_params=_cparams(("parallel", "arbitrary")),
        name="inproj",
    )(x2d, gain, w_main, w_gate)


def _hgrn_chunk_fn(q_ref, f_ref, v_ref, g_ref, lb_ref, gain_ref, tril_ref, o_ref, st_ref):
    lbr = lb_ref[...]
    mx = jnp.maximum(lbr[0:1], lbr[1:2])
    e0 = jnp.exp(lbr[0:1] - mx)
    e1 = jnp.exp(lbr[1:2] - mx)
    lb = e0 / (e0 + e1)
    one_m_lb = 1.0 - lb
    tril = tril_ref[...]
    gain = gain_ref[...]
    rows = lax.broadcasted_iota(I32, (CHUNK, CHUNK), 0)
    cols = lax.broadcasted_iota(I32, (CHUNK, CHUNK), 1)
    causal = rows >= cols

    def chunk(c):
        r0 = pl.multiple_of(c * CHUNK, CHUNK)
        q = q_ref[pl.ds(r0, CHUNK), :].astype(F32)
        fl = f_ref[pl.ds(r0, CHUNK), :].astype(F32)
        log_f = jnp.log(lb + one_m_lb * _sigmoid(fl))
        k = one_m_lb * _sigmoid(-fl)
        qs = q * _sigmoid(q)
        bc = _cumsum_rows(tril, log_f)
        b_ref = bc[CHUNK // 2:CHUNK // 2 + 1, :]
        b_end = bc[CHUNK - 1:CHUNK, :]
        qd = (qs * jnp.exp(bc - b_ref)).astype(BF16)
        kd = (k * jnp.exp(b_ref - bc)).astype(BF16)
        ke = (k * jnp.exp(b_end - bc)).astype(BF16)
        qe = (qs * jnp.exp(bc)).astype(BF16)
        decay = jnp.exp(b_end)
        for h in range(H_A):
            sl = slice(h * DK_A, (h + 1) * DK_A)
            scores = _dot_nt(qd[:, sl], kd[:, sl])
            scores = jnp.where(causal, scores, 0.0).astype(BF16)
            vh = v_ref[pl.ds(r0, CHUNK), sl]
            st = st_ref[h]
            o = jnp.dot(scores, vh, preferred_element_type=F32)
            o = o + _dot_nt(qe[:, sl], st.astype(BF16))
            u_t = _dot_tn(vh, ke[:, sl])
            st_ref[h] = st * decay[:, sl] + u_t
            ms = jnp.mean(o * o, axis=-1, keepdims=True)
            gt = g_ref[pl.ds(r0, CHUNK), sl].astype(F32)
            y = o * lax.rsqrt(ms + EPS) * gain[:, sl] * (gt * _sigmoid(gt))
            o_ref[pl.ds(r0, CHUNK), sl] = y.astype(BF16)

    return chunk


def _log_sigmoid(x):
    return jnp.minimum(x, 0.0) - jnp.log1p(jnp.exp(-jnp.abs(x)))


def _mlstm_chunk_fn(qk_ref, v_ref, o_in_ref, gc_ref, cw_ref, cb_ref, gb_ref, gain_ref, tril_ref,
                    o_ref, c_ref, n_ref, m_ref, tail_ref, ext_s):
    tril = tril_ref[...]
    gain = gain_ref[...]
    cw = cw_ref[...]
    cb = cb_ref[...]
    gb = gb_ref[...]
    rows = lax.broadcasted_iota(I32, (CHUNK, CHUNK), 0)
    cols = lax.broadcasted_iota(I32, (CHUNK, CHUNK), 1)
    causal = rows >= cols
    hq = H_B * DQK_B
    scale = DQK_B ** -0.5

    def chunk(c):
        r0 = pl.multiple_of(c * CHUNK, CHUNK)
        u = qk_ref[pl.ds(r0, CHUNK), :].astype(F32)
        ext_s[0:8, :] = tail_ref[...]
        ext_s[8:8 + CHUNK, :] = u
        tail_ref[...] = u[CHUNK - 8:CHUNK, :]
        conv = cb + cw[CONV_W - 1:CONV_W, :] * u
        for j in range(CONV_W - 1):
            lag = CONV_W - 1 - j
            conv = conv + cw[j:j + 1, :] * ext_s[8 - lag:8 - lag + CHUNK, :]
        qk = conv * _sigmoid(conv)
        q = qk[:, :hq]
        k = qk[:, hq:] * scale

        g = gc_ref[pl.ds(r0, CHUNK), :] + gb
        fc = _cumsum_rows(tril, _log_sigmoid(g))
        zpad = jnp.zeros((LANES - CHUNK, LANES), F32)
        g_t = jnp.concatenate([g, zpad], axis=0).T
        fc_t = jnp.concatenate([fc, zpad], axis=0).T

        for h in range(H_B):
            ig_c = g[:, h:h + 1]
            fc_c = fc[:, H_B + h:H_B + h + 1]
            ig_r = g_t[h:h + 1, 0:CHUNK]
            fc_r = fc_t[H_B + h:H_B + h + 1, 0:CHUNK]
            g_tot = fc_c[CHUNK - 1:CHUNK, :]
            qh = q[:, h * DQK_B:(h + 1) * DQK_B]
            kh = k[:, h * DQK_B:(h + 1) * DQK_B]
            vh = v_ref[pl.ds(r0, CHUNK), h * DV_B:(h + 1) * DV_B]
            qb = qh.astype(BF16)
            c_st = c_ref[h]
            n_st = n_ref[h]
            m_st = m_ref[h][0:1, 0:1]

            d_log = jnp.where(causal, fc_c - fc_r + ig_r, -jnp.inf)
            a_int = fc_c + m_st
            m_t = jnp.maximum(jnp.max(d_log, axis=-1, keepdims=True), a_int)
            w_in = jnp.exp(d_log - m_t)
            w_state = jnp.exp(a_int - m_t)
            qkw = _dot_nt(qb, kh.astype(BF16)) * w_in
            num = (jnp.dot(qkw.astype(BF16), vh, preferred_element_type=F32)
                   + w_state * jnp.dot(qb, c_st.astype(BF16), preferred_element_type=F32))
            den = (jnp.sum(qkw, axis=-1, keepdims=True)
                   + w_state * jnp.sum(qh * n_st, axis=-1, keepdims=True))
            hh = num / jnp.maximum(jnp.abs(den), jnp.exp(-m_t))

            a_end = g_tot - fc_c + ig_c
            m_loc = jnp.max(a_end, axis=0, keepdims=True)
            kw = kh * jnp.exp(a_end - m_loc)
            c_loc = _dot_tn(kw.astype(BF16), vh)
            n_loc = jnp.sum(kw, axis=0, keepdims=True)
            m_new = jnp.maximum(g_tot + m_st, m_loc)
            s_prev = jnp.exp(g_tot + m_st - m_new)
            s_loc = jnp.exp(m_loc - m_new)
            c_ref[h] = s_prev * c_st + s_loc * c_loc
            n_ref[h] = s_prev * n_st + s_loc * n_loc
            m_ref[h] = jnp.broadcast_to(m_new, m_ref.shape[1:])

            vs = slice(h * DV_B, (h + 1) * DV_B)
            ms = jnp.mean(hh * hh, axis=-1, keepdims=True)
            og = o_in_ref[pl.ds(r0, CHUNK), vs].astype(F32)
            y = hh * lax.rsqrt(ms + EPS) * gain[:, vs] * _sigmoid(og)
            o_ref[pl.ds(r0, CHUNK), vs] = y.astype(BF16)

    return chunk


def _mixers_kernel(q_ref, f_ref, va_ref, g_ref, lb_ref, gain_a_ref,
                   qk_ref, vb_ref, ob_ref, gc_ref, cw_ref, cb_ref, gb_ref, gain_b_ref, tril_ref,
                   ya_ref, yb_ref, st_ref, c_ref, n_ref, m_ref, tail_ref, ext_s):
    @pl.when(pl.program_id(1) == 0)
    def _():
        st_ref[...] = jnp.zeros_like(st_ref)
        c_ref[...] = jnp.zeros_like(c_ref)
        n_ref[...] = jnp.zeros_like(n_ref)
        m_ref[...] = jnp.full_like(m_ref, M_INIT)
        tail_ref[...] = jnp.zeros_like(tail_ref)

    hgrn_chunk = _hgrn_chunk_fn(q_ref, f_ref, va_ref, g_ref, lb_ref, gain_a_ref, tril_ref,
                                ya_ref, st_ref)
    mlstm_chunk = _mlstm_chunk_fn(qk_ref, vb_ref, ob_ref, gc_ref, cw_ref, cb_ref, gb_ref,
                                  gain_b_ref, tril_ref, yb_ref, c_ref, n_ref, m_ref, tail_ref,
                                  ext_s)

    def body(c, carry):
        hgrn_chunk(c)
        mlstm_chunk(c)
        return carry

    lax.fori_loop(0, q_ref.shape[0] // CHUNK, body, 0)


def _mixers(z3, gc3, lb_raw, gain_a, conv_w, conv_b, gate_bias, gain_b, tril, ts):
    b, s, _ = z3.shape
    w = H_A * DK_A
    assert w == H_B * DV_B == 2 * H_B * DQK_B, "column blocks of z share one width"

    def zspec(col):
        return pl.BlockSpec((None, ts, w), lambda i, j, col=col: (i, j, col))

    def const(shape):
        return pl.BlockSpec(shape, lambda i, j: (0,) * len(shape))

    out = jax.ShapeDtypeStruct((b, s, w), BF16)
    out_spec = pl.BlockSpec((None, ts, w), lambda i, j: (i, j, 0))
    return pl.pallas_call(
        _mixers_kernel,
        out_shape=(out, out),
        grid=(b, s // ts),
        in_specs=[zspec(0), zspec(1), zspec(2), zspec(3), const((2, w)), const((1, w)),
                  zspec(4), zspec(5), zspec(6),
                  pl.BlockSpec((None, ts, LANES), lambda i, j: (i, j, 0)),
                  const((CONV_W, w)), const((1, w)), const((1, LANES)), const((1, w)),
                  const((CHUNK, CHUNK))],
        out_specs=(out_spec, out_spec),
        scratch_shapes=[pltpu.VMEM((H_A, DV_A, DK_A), F32),
                        pltpu.VMEM((H_B, DQK_B, DV_B), F32),
                        pltpu.VMEM((H_B, 1, DQK_B), F32),
                        pltpu.VMEM((H_B, 8, LANES), F32),
                        pltpu.VMEM((8, w), F32),
                        pltpu.VMEM((8 + CHUNK, w), F32)],
        compiler_params=_cparams(("parallel", "arbitrary")),
        name="mixers",
    )(z3, z3, z3, z3, lb_raw, gain_a, z3, z3, z3, gc3, conv_w, conv_b, gate_bias, gain_b, tril)


def _merge_kernel(ya_ref, yb_ref, ra0_ref, ra1_ref, rb0_ref, rb1_ref, x_ref,
                  wa_ref, wb_ref, wo_ref, g_ref, rw_ref, rb_ref, upper_ref,
                  x1_ref, hp_ref, idx_ref, gate_ref, gcol_ref, rank_ref, cnt_ref, carry_ref):
    @pl.when(pl.program_id(0) == 0)
    def _():
        carry_ref[...] = jnp.zeros_like(carry_ref)

    half = wo_ref.shape[0] // 2
    pa = jnp.dot(ya_ref[...], wa_ref[...], preferred_element_type=F32)
    pb = jnp.dot(yb_ref[...], wb_ref[...], preferred_element_type=F32)
    mix0 = (_sigmoid(ra0_ref[...].astype(F32)) * pa[:, :half]
            + _sigmoid(rb0_ref[...].astype(F32)) * pb[:, :half]).astype(BF16)
    mix1 = (_sigmoid(ra1_ref[...].astype(F32)) * pa[:, half:]
            + _sigmoid(rb1_ref[...].astype(F32)) * pb[:, half:]).astype(BF16)
    x1 = (x_ref[...] + jnp.dot(mix0, wo_ref[0:half, :], preferred_element_type=F32)
          + jnp.dot(mix1, wo_ref[half:, :], preferred_element_type=F32))
    x1_ref[...] = x1

    ms = jnp.mean(x1 * x1, axis=-1, keepdims=True)
    h2 = x1 * lax.rsqrt(ms + EPS) * g_ref[...]
    hb = h2.astype(BF16)
    lo = pltpu.bitcast(hb[:, :half].astype(F32), U32) >> 16
    hi = pltpu.bitcast(hb[:, half:].astype(F32), U32) & jnp.uint32(0xFFFF0000)
    hp_ref[...] = lo | hi

    h_lo = (h2 - hb.astype(F32)).astype(BF16)
    rw = rw_ref[...]
    rw_hi = rw.astype(BF16)
    rw_lo = (rw - rw_hi.astype(F32)).astype(BF16)
    logits = (_dot_nt(rw_hi, hb) + _dot_nt(rw_hi, h_lo) + _dot_nt(rw_lo, hb)) + rb_ref[...]

    tm = logits.shape[1]
    erow = lax.broadcasted_iota(I32, (N_EXPERTS, tm), 0)
    work = logits
    vals, idxs = [], []
    for _ in range(TOP_K):
        mval = jnp.max(work, axis=0, keepdims=True)
        idx = jnp.min(jnp.where(work == mval, erow, N_EXPERTS), axis=0, keepdims=True)
        work = jnp.where(erow == idx, -jnp.inf, work)
        vals.append(mval)
        idxs.append(idx)
    exps = [jnp.exp(v - vals[0]) for v in vals]
    denom = exps[0] + exps[1] + exps[2] + exps[3]
    gates = [e / denom for e in exps]

    onehots = [(erow == idx).astype(F32) for idx in idxs]
    mask = onehots[0] + onehots[1] + onehots[2] + onehots[3]
    before = jnp.dot(mask.astype(BF16), upper_ref[...], preferred_element_type=F32)
    before = before + carry_ref[:, 0:1]
    carry = carry_ref[...] + jnp.sum(mask, axis=1, keepdims=True)
    carry_ref[...] = carry
    cnt_ref[...] = carry

    idx_ref[...] = jnp.concatenate(idxs, axis=0)
    gate_rows = jnp.concatenate(gates, axis=0)
    gate_ref[...] = gate_rows
    rank_ref[...] = jnp.concatenate(
        [jnp.sum(oh * before, axis=0, keepdims=True) for oh in onehots], axis=0).astype(I32)
    padded = jnp.concatenate([gate_rows, jnp.zeros((LANES - TOP_K, tm), F32)], axis=0)
    gcol_ref[...] = padded.T


def _merge(ya, yb, z, x2d, wa, wb, wo, gain, rw_t, rb_col, upper, tm):
    t, d = x2d.shape
    half = d // 2
    ra_blk = (4 * H_A * DK_A + 3 * H_B * DV_B) // half

    def rows(width, col=0):
        return pl.BlockSpec((tm, width), lambda i, col=col: (i, col))

    def const(shape):
        return pl.BlockSpec(shape, lambda i: (0,) * len(shape), pipeline_mode=pl.Buffered(1))

    out_shape = (
        jax.ShapeDtypeStruct((t, d), F32),
        jax.ShapeDtypeStruct((t, half), U32),
        jax.ShapeDtypeStruct((TOP_K, t), I32),
        jax.ShapeDtypeStruct((TOP_K, t), F32),
        jax.ShapeDtypeStruct((t, LANES), F32),
        jax.ShapeDtypeStruct((TOP_K, t), I32),
        jax.ShapeDtypeStruct((N_EXPERTS, LANES), F32),
    )
    return pl.pallas_call(
        _merge_kernel,
        out_shape=out_shape,
        grid=(t // tm,),
        in_specs=[rows(ya.shape[1]), rows(yb.shape[1]),
                  rows(half, ra_blk), rows(half, ra_blk + 1), rows(half, ra_blk + 2),
                  rows(half, ra_blk + 3), rows(d),
                  const(wa.shape), const(wb.shape), const(wo.shape), const((1, d)),
                  const(rw_t.shape), const((N_EXPERTS, 1)), const((tm, tm))],
        out_specs=(rows(d), rows(half),
                   pl.BlockSpec((TOP_K, tm), lambda i: (0, i)),
                   pl.BlockSpec((TOP_K, tm), lambda i: (0, i)),
                   rows(LANES),
                   pl.BlockSpec((TOP_K, tm), lambda i: (0, i)),
                   pl.BlockSpec((N_EXPERTS, LANES), lambda i: (0, 0))),
        scratch_shapes=[pltpu.VMEM((N_EXPERTS, LANES), F32)],
        compiler_params=_cparams(("arbitrary",)),
        name="merge_router",
    )(ya, yb, z, z, z, z, x2d, wa, wb, wo, gain, rw_t, rb_col, upper)


def _dispatch_kernel(pad_end_ref, padded_ref, pos_ref, src_ref, dst_ref, zero_s, zsem, sem):
    td = src_ref.shape[0]

    @pl.when(pl.program_id(0) == 0)
    def _():
        zero_s[...] = jnp.zeros_like(zero_s)

        def tail_copy(e):
            start = pl.multiple_of(pad_end_ref[e] - EXPERT_TILE, EXPERT_TILE)
            return pltpu.make_async_copy(zero_s, dst_ref.at[pl.ds(start, EXPERT_TILE)], zsem)

        for e in range(N_EXPERTS):
            @pl.when(padded_ref[e] > 0)
            def _():
                tail_copy(e).start()
        for e in range(N_EXPERTS):
            @pl.when(padded_ref[e] > 0)
            def _():
                tail_copy(e).wait()

    def issue(r, carry):
        for k in range(TOP_K):
            pltpu.make_async_copy(src_ref.at[pl.ds(r, 1)],
                                  dst_ref.at[pl.ds(pos_ref[k, r], 1)], sem).start(priority=k % 2)
        return carry

    lax.fori_loop(0, td, issue, 0, unroll=8)
    for k in range(TOP_K):
        pltpu.make_async_copy(src_ref, dst_ref.at[pl.ds(0, td)], sem).wait()


def _dispatch(pad_end, padded, pos, src, n_rows, td):
    t, d = src.shape
    grid_spec = pltpu.PrefetchScalarGridSpec(
        num_scalar_prefetch=2,
        grid=(t // td,),
        in_specs=[pl.BlockSpec((TOP_K, td), lambda i, pe, pd: (0, i), memory_space=pltpu.SMEM),
                  pl.BlockSpec((td, d), lambda i, pe, pd: (i, 0))],
        out_specs=pl.BlockSpec(memory_space=pl.ANY),
        scratch_shapes=[pltpu.VMEM((EXPERT_TILE, d), src.dtype),
                        pltpu.SemaphoreType.DMA, pltpu.SemaphoreType.DMA],
    )
    return pl.pallas_call(
        _dispatch_kernel,
        out_shape=jax.ShapeDtypeStruct((n_rows, d), src.dtype),
        grid_spec=grid_spec,
        compiler_params=pltpu.CompilerParams(dimension_semantics=("arbitrary",),
                                             vmem_limit_bytes=V7X_VMEM_LIMIT_BYTES,
                                             has_side_effects=True),
        name="dispatch",
    )(pad_end, padded, pos, src)


def _expert_kernel(te_ref, tv_ref, xs_ref, wgu_hbm, wdn_hbm, bgu_ref, bdn_ref, o_ref,
                   wgu_s, wdn_s, stage_s, sems, xb_s, act_s, cur_ref):
    i = pl.program_id(0)
    valid = tv_ref[i] > 0
    e = te_ref[i]
    d, two_ff = wgu_s.shape
    d_ff = two_ff // 2
    half = d // 2
    n_slots, rows_per, _ = stage_s.shape
    n_gu = 2 * (d // rows_per)
    n_all = n_gu + d_ff // rows_per

    @pl.when(i == 0)
    def _():
        cur_ref[0] = -1

    @pl.when(jnp.logical_and(valid, cur_ref[0] != e))
    def _():
        def piece(j):
            if j < n_gu:
                return j // 2, j % 2
            return j - n_gu, 0

        def chunk_copy(j):
            r, c = piece(j)
            rows = pl.ds(r * rows_per, rows_per)
            if j < n_gu:
                src = wgu_hbm.at[e, rows, pl.ds(c * d_ff, d_ff)]
            else:
                src = wdn_hbm.at[e, rows, :]
            return pltpu.make_async_copy(src, stage_s.at[j % n_slots], sems.at[j % n_slots])

        for j in range(n_slots - 1):
            chunk_copy(j).start()
        for j in range(n_all):
            if j + n_slots - 1 < n_all:
                chunk_copy(j + n_slots - 1).start()
            chunk_copy(j).wait()
            w = stage_s[j % n_slots].astype(BF16)
            r, c = piece(j)
            if j < n_gu:
                wgu_s[r * rows_per:(r + 1) * rows_per, c * d_ff:(c + 1) * d_ff] = w
            else:
                wdn_s[r * rows_per:(r + 1) * rows_per, :] = w
        cur_ref[0] = e

    @pl.when(valid)
    def _():
        w = xs_ref[...]
        xb_s[:, 0:half] = pltpu.bitcast(w << 16, F32).astype(BF16)
        xb_s[:, half:] = pltpu.bitcast(w & jnp.uint32(0xFFFF0000), F32).astype(BF16)
        xb = xb_s[...]
        for c in range(d_ff // FF_CHUNK):
            ca = slice(c * FF_CHUNK, (c + 1) * FF_CHUNK)
            cu = slice(d_ff + c * FF_CHUNK, d_ff + (c + 1) * FF_CHUNK)
            a = jnp.dot(xb, wgu_s[:, ca], preferred_element_type=F32) + bgu_ref[:, ca]
            u = jnp.dot(xb, wgu_s[:, cu], preferred_element_type=F32) + bgu_ref[:, cu]
            a = jnp.minimum(a, SWIGLU_LIMIT)
            u = jnp.clip(u, -SWIGLU_LIMIT, SWIGLU_LIMIT)
            act_s[:, ca] = ((u + 1.0) * a * _sigmoid(SWIGLU_ALPHA * a)).astype(BF16)
        o_ref[...] = jnp.dot(act_s[...], wdn_s[...], preferred_element_type=F32) + bdn_ref[...]

    @pl.when(jnp.logical_not(valid))
    def _():
        o_ref[...] = jnp.zeros_like(o_ref)


def _experts(tile_expert, tile_valid, xs, w_gu, b_gu, w_dn, b_dn):
    n_rows, half = xs.shape
    d = 2 * half
    d_ff = w_dn.shape[1]
    n_tiles = n_rows // EXPERT_TILE

    grid_spec = pltpu.PrefetchScalarGridSpec(
        num_scalar_prefetch=2,
        grid=(n_tiles,),
        in_specs=[
            pl.BlockSpec((EXPERT_TILE, half), lambda i, te, tv: (jnp.minimum(i, tv[n_tiles]), 0)),
            pl.BlockSpec(memory_space=pl.ANY),
            pl.BlockSpec(memory_space=pl.ANY),
            pl.BlockSpec((None, 1, 2 * d_ff), lambda i, te, tv: (te[i], 0, 0)),
            pl.BlockSpec((None, 1, d), lambda i, te, tv: (te[i], 0, 0)),
        ],
        out_specs=pl.BlockSpec((EXPERT_TILE, d), lambda i, te, tv: (i, 0)),
        scratch_shapes=[pltpu.VMEM((d, 2 * d_ff), BF16),
                        pltpu.VMEM((d_ff, d), BF16),
                        pltpu.VMEM((WEIGHT_RING, WEIGHT_ROWS, d_ff), F32),
                        pltpu.SemaphoreType.DMA((WEIGHT_RING,)),
                        pltpu.VMEM((EXPERT_TILE, d), BF16),
                        pltpu.VMEM((EXPERT_TILE, d_ff), BF16),
                        pltpu.SMEM((1,), I32)],
    )
    assert d == d_ff, "weight staging shares one (WEIGHT_ROWS, d_ff) buffer for w_gu and w_dn pieces"
    return pl.pallas_call(
        _expert_kernel,
        out_shape=jax.ShapeDtypeStruct((n_rows, d), F32),
        grid_spec=grid_spec,
        compiler_params=_cparams(("arbitrary",)),
        name="experts",
    )(tile_expert, tile_valid, xs, w_gu, w_dn, b_gu, b_dn)


def _combine_kernel(pos_ref, posn_ref, ys_ref, x1_ref, gcol_ref, p_ref, g_ple_ref, wpg_ref,
                    wple_ref, g_fin_ref, o_ref, buf, sems):
    tc = x1_ref.shape[0]
    i = pl.program_id(0)
    slot = lax.rem(i, 2)

    def gather(idx_ref, dst_slot):
        def issue(r, carry):
            for k in range(TOP_K):
                pltpu.make_async_copy(ys_ref.at[pl.ds(idx_ref[k, r], 1)],
                                      buf.at[dst_slot, k, pl.ds(r, 1)],
                                      sems.at[dst_slot]).start(priority=k % 2)
            return carry
        lax.fori_loop(0, tc, issue, 0, unroll=True)

    @pl.when(i == 0)
    def _():
        gather(pos_ref, slot)

    gather(posn_ref, 1 - slot)

    def wait_slot(s):
        for k in range(TOP_K):
            pltpu.make_async_copy(ys_ref.at[pl.ds(0, tc)], buf.at[s, k], sems.at[s]).wait()

    ple = jnp.dot(p_ref[...].astype(BF16), wple_ref[...], preferred_element_type=F32)
    wait_slot(slot)

    @pl.when(i + 1 == pl.num_programs(0))
    def _():
        wait_slot(1 - slot)

    gcol = gcol_ref[...]
    moe = buf[slot, 0] * gcol[:, 0:1]
    for k in range(1, TOP_K):
        moe = moe + buf[slot, k] * gcol[:, k:k + 1]
    x2 = x1_ref[...] + moe
    ms = jnp.mean(x2 * x2, axis=-1, keepdims=True)
    h3 = (x2 * lax.rsqrt(ms + EPS) * g_ple_ref[...]).astype(BF16)
    gate = _sigmoid(jnp.dot(h3, wpg_ref[...], preferred_element_type=F32))
    x3 = x2 + ple * gate
    ms3 = jnp.mean(x3 * x3, axis=-1, keepdims=True)
    o_ref[...] = x3 * lax.rsqrt(ms3 + EPS) * g_fin_ref[...]


def _combine(pos, ys, x1, gcol, p2d, g_ple, wpg, wple, g_fin, tc):
    t, d = x1.shape
    n_steps = t // tc

    def rows(width):
        return pl.BlockSpec((tc, width), lambda i: (i, 0))

    def const(shape):
        return pl.BlockSpec(shape, lambda i: (0,) * len(shape), pipeline_mode=pl.Buffered(1))

    return pl.pallas_call(
        _combine_kernel,
        out_shape=jax.ShapeDtypeStruct((t, d), F32),
        grid=(t // tc,),
        in_specs=[pl.BlockSpec((TOP_K, tc), lambda i: (0, i), memory_space=pltpu.SMEM),
                  pl.BlockSpec((TOP_K, tc), lambda i: (0, jnp.minimum(i + 1, n_steps - 1)),
                               memory_space=pltpu.SMEM),
                  pl.BlockSpec(memory_space=pl.ANY),
                  rows(d), rows(LANES), rows(p2d.shape[1]),
                  const((1, d)), const(wpg.shape), const(wple.shape), const((1, d))],
        out_specs=rows(d),
        scratch_shapes=[pltpu.VMEM((2, TOP_K, tc, d), F32), pltpu.SemaphoreType.DMA((2,))],
        compiler_params=_cparams(("arbitrary",)),
        name="combine_ple",
    )(pos, pos, ys, x1, gcol, p2d, g_ple, wpg, wple, g_fin)


def _pick(n, pref):
    return pref if n % pref == 0 else n


def kernel(x, p, norm_mix, w_in, hgrn_lb, hgrn_norm, mlstm_conv_w, mlstm_conv_b, mlstm_b_i,
           mlstm_b_f, mlstm_norm, w_branch_a, w_branch_b, w_out, norm_moe, router_w, router_b,
           exp_w_gu, exp_b_gu, exp_w_dn, exp_b_dn, norm_ple, w_ple, w_ple_gate, norm_final):
    b, s, d = x.shape
    t = b * s
    depth = norm_mix.shape[0]
    wa_cols = H_A * DK_A
    wb_cols = H_B * DV_B
    g0 = 4 * wa_cols + 2 * wb_cols
    g1 = g0 + 2 * H_B
    tril = jnp.tril(jnp.ones((CHUNK, CHUNK), F32)).astype(BF16)
    x2d = x.reshape(t, d)

    for i in range(depth):
        w_i = w_in[i]
        w_main = jnp.concatenate([w_i[:, :g0], w_i[:, g1:]], axis=1).astype(BF16)
        w_gate = jnp.pad(w_i[:, g0:g1], ((0, 0), (0, LANES - 2 * H_B))).astype(BF16)
        gate_bias = jnp.pad(jnp.concatenate([mlstm_b_i[i], mlstm_b_f[i]]),
                            (0, LANES - 2 * H_B)).reshape(1, LANES)

        tm_in = _pick(t, 1024)
        tn_in = _pick(w_main.shape[1], 1408)
        z, gc = _inproj(x2d, norm_mix[i].reshape(1, d), w_main, w_gate, tm_in, tn_in)
        z3 = z.reshape(b, s, z.shape[1])
        gc3 = gc.reshape(b, s, LANES)

        ts = _pick(s, 1024)
        ya, yb = _mixers(z3, gc3, hgrn_lb[i:i + 2], hgrn_norm[i].reshape(1, wa_cols),
                         mlstm_conv_w[i], mlstm_conv_b[i].reshape(1, -1), gate_bias,
                         mlstm_norm[i].reshape(1, wb_cols), tril, ts)

        tm_mg = _pick(t, 512)
        upper = jnp.triu(jnp.ones((tm_mg, tm_mg), F32), 1).astype(BF16)
        x1, hp, idx, gate_rows, gcol, rank, cnt = _merge(
            ya.reshape(t, wa_cols), yb.reshape(t, wb_cols), z, x2d,
            w_branch_a[i].astype(BF16), w_branch_b[i].astype(BF16), w_out[i].astype(BF16),
            norm_moe[i].reshape(1, d), router_w[i].T, router_b[i].reshape(N_EXPERTS, 1),
            upper, tm_mg)
        del gate_rows

        counts = cnt[:, 0].astype(I32)
        padded = (counts + EXPERT_TILE - 1) // EXPERT_TILE * EXPERT_TILE
        pad_end = jnp.cumsum(padded)
        pad_start = pad_end - padded
        e_ids = jnp.arange(N_EXPERTS, dtype=I32)
        pos = rank + jnp.sum(jnp.where(idx[..., None] == e_ids, pad_start, 0), axis=-1)
        n_rows = t * TOP_K + N_EXPERTS * EXPERT_TILE
        n_tiles = n_rows // EXPERT_TILE
        tile_row = jnp.arange(n_tiles, dtype=I32) * EXPERT_TILE
        n_valid = pad_end[-1] // EXPERT_TILE
        last_valid = jnp.maximum(n_valid - 1, 0)
        tile_valid = (tile_row < pad_end[-1]).astype(I32)
        tile_expert = jnp.sum((tile_row[:, None] >= pad_end[None, :]).astype(I32), axis=1)
        tile_expert = jnp.minimum(tile_expert, N_EXPERTS - 1)
        last_e = jnp.sum(jnp.where(jnp.arange(n_tiles) == last_valid, tile_expert, 0))
        tile_expert = jnp.where(tile_valid > 0, tile_expert, last_e).astype(I32)
        tile_valid = jnp.concatenate([tile_valid, last_valid.reshape(1).astype(I32)])

        xs = _dispatch(pad_end.astype(I32), padded, pos, hp, n_rows, _pick(t, 512))
        ys = _experts(tile_expert, tile_valid, xs,
                      exp_w_gu[i], exp_b_gu[i].reshape(N_EXPERTS, 1, -1),
                      exp_w_dn[i], exp_b_dn[i].reshape(N_EXPERTS, 1, d))

        is_last = i == depth - 1
        assert is_last, "the fused final-norm epilogue assumes a single layer"
        x2d = _combine(pos, ys, x1, gcol, p[i].reshape(t, -1), norm_ple[i].reshape(1, d),
                       w_ple_gate[i].astype(BF16), w_ple[i].astype(BF16),
                       norm_final.reshape(1, d), _pick(t, 256))
    return x2d.reshape(b, s, d)
```

```python
import jax
import jax.numpy as jnp
from jax import lax
from jax.experimental import pallas as pl
from jax.experimental.pallas import tpu as pltpu

F32 = jnp.float32
BF16 = jnp.bfloat16
U32 = jnp.uint32
I32 = jnp.int32

EPS = 1e-6
CHUNK = 64
H_A, DK_A, DV_A = 8, 128, 128
H_B, DQK_B, DV_B = 4, 128, 256
CONV_W = 4
M_INIT = -1e30
N_EXPERTS = 32
TOP_K = 4
SWIGLU_LIMIT = 7.0
SWIGLU_ALPHA = 1.702

V7X_VMEM_LIMIT_BYTES = 56 * 1024 * 1024
LANES = 128

EXPERT_TILE = 512
FF_CHUNK = 512
WEIGHT_ROWS = 256
WEIGHT_RING = 6


def _cparams(sem):
    return pltpu.CompilerParams(dimension_semantics=sem, vmem_limit_bytes=V7X_VMEM_LIMIT_BYTES)


def _sigmoid(x):
    return jax.nn.sigmoid(x)


def _split3(x):
    hi = x.astype(BF16)
    r1 = x - hi.astype(F32)
    mid = r1.astype(BF16)
    lo = (r1 - mid.astype(F32)).astype(BF16)
    return hi, mid, lo


def _cumsum_rows(tril, x):
    hi, mid, lo = _split3(x)
    acc = jnp.dot(tril, hi, preferred_element_type=F32)
    acc = acc + jnp.dot(tril, mid, preferred_element_type=F32)
    return acc + jnp.dot(tril, lo, preferred_element_type=F32)


def _dot_tn(a, b):
    return lax.dot_general(a, b, (((0,), (0,)), ((), ())), preferred_element_type=F32)


def _dot_nt(a, b):
    return lax.dot_general(a, b, (((1,), (1,)), ((), ())), preferred_element_type=F32)


def _inproj_kernel(x_ref, g_ref, w_ref, wg_ref, z_ref, gc_ref, h_scr):
    @pl.when(pl.program_id(1) == 0)
    def _():
        x = x_ref[...]
        ms = jnp.mean(x * x, axis=-1, keepdims=True)
        hb = (x * lax.rsqrt(ms + EPS) * g_ref[...]).astype(BF16)
        h_scr[...] = hb
        gc_ref[...] = jnp.dot(hb, wg_ref[...], preferred_element_type=F32)

    z_ref[...] = jnp.dot(h_scr[...], w_ref[...], preferred_element_type=F32).astype(BF16)


def _inproj(x2d, gain, w_main, w_gate, tm, tn):
    t, d = x2d.shape
    n = w_main.shape[1]
    return pl.pallas_call(
        _inproj_kernel,
        out_shape=(jax.ShapeDtypeStruct((t, n), BF16), jax.ShapeDtypeStruct((t, LANES), F32)),
        grid=(t // tm, n // tn),
        in_specs=[
            pl.BlockSpec((tm, d), lambda i, j: (i, 0)),
            pl.BlockSpec((1, d), lambda i, j: (0, 0)),
            pl.BlockSpec((d, tn), lambda i, j: (0, j)),
            pl.BlockSpec((d, LANES), lambda i, j: (0, 0)),
        ],
        out_specs=(
            pl.BlockSpec((tm, tn), lambda i, j: (i, j)),
            pl.BlockSpec((tm, LANES), lambda i, j: (i, 0)),
        ),
        scratch_shapes=[pltpu.VMEM((tm, d), BF16)],
        compiler_params=_cparams(("parallel", "arbitrary")),
        name="inproj",
    )(x2d, gain, w_main, w_gate)


def _hgrn_chunk_fn(q_ref, f_ref, v_ref, g_ref, lb_ref, gain_ref, tril_ref, o_ref, st_ref):
    lbr = lb_ref[...]
    mx = jnp.maximum(lbr[0:1], lbr[1:2])
    e0 = jnp.exp(lbr[0:1] - mx)
    e1 = jnp.exp(lbr[1:2] - mx)
    lb = e0 / (e0 + e1)
    one_m_lb = 1.0 - lb
    tril = tril_ref[...]
    gain = gain_ref[...]
    rows = lax.broadcasted_iota(I32, (CHUNK, CHUNK), 0)
    cols = lax.broadcasted_iota(I32, (CHUNK, CHUNK), 1)
    causal = rows >= cols

    def chunk(c):
        r0 = pl.multiple_of(c * CHUNK, CHUNK)
        q = q_ref[pl.ds(r0, CHUNK), :].astype(F32)
        fl = f_ref[pl.ds(r0, CHUNK), :].astype(F32)
        log_f = jnp.log(lb + one_m_lb * _sigmoid(fl))
        k = one_m_lb * _sigmoid(-fl)
        qs = q * _sigmoid(q)
        bc = _cumsum_rows(tril, log_f)
        b_ref = bc[CHUNK // 2:CHUNK // 2 + 1, :]
        b_end = bc[CHUNK - 1:CHUNK, :]
        qd = (qs * jnp.exp(bc - b_ref)).astype(BF16)
        kd = (k * jnp.exp(b_ref - bc)).astype(BF16)
        ke = (k * jnp.exp(b_end - bc)).astype(BF16)
        qe = (qs * jnp.exp(bc)).astype(BF16)
        decay = jnp.exp(b_end)
        for h in range(H_A):
            sl = slice(h * DK_A, (h + 1) * DK_A)
            scores = _dot_nt(qd[:, sl], kd[:, sl])
            scores = jnp.where(causal, scores, 0.0).astype(BF16)
            vh = v_ref[pl.ds(r0, CHUNK), sl]
            st = st_ref[h]
            o = jnp.dot(scores, vh, preferred_element_type=F32)
            o = o + _dot_nt(qe[:, sl], st.astype(BF16))
            u_t = _dot_tn(vh, ke[:, sl])
            st_ref[h] = st * decay[:, sl] + u_t
            ms = jnp.mean(o * o, axis=-1, keepdims=True)
            gt = g_ref[pl.ds(r0, CHUNK), sl].astype(F32)
            y = o * lax.rsqrt(ms + EPS) * gain[:, sl] * (gt * _sigmoid(gt))
            o_ref[pl.ds(r0, CHUNK), sl] = y.astype(BF16)

    return chunk


def _log_sigmoid(x):
    return jnp.minimum(x, 0.0) - jnp.log1p(jnp.exp(-jnp.abs(x)))


def _mlstm_chunk_fn(qk_ref, v_ref, o_in_ref, gc_ref, cw_ref, cb_ref, gb_ref, gain_ref, tril_ref,
                    o_ref, c_ref, n_ref, m_ref, tail_ref, ext_s):
    tril = tril_ref[...]
    gain = gain_ref[...]
    cw = cw_ref[...]
    cb = cb_ref[...]
    gb = gb_ref[...]
    rows = lax.broadcasted_iota(I32, (CHUNK, CHUNK), 0)
    cols = lax.broadcasted_iota(I32, (CHUNK, CHUNK), 1)
    causal = rows >= cols
    hq = H_B * DQK_B
    scale = DQK_B ** -0.5

    def chunk(c):
        r0 = pl.multiple_of(c * CHUNK, CHUNK)
        u = qk_ref[pl.ds(r0, CHUNK), :].astype(F32)
        ext_s[0:8, :] = tail_ref[...]
        ext_s[8:8 + CHUNK, :] = u
        tail_ref[...] = u[CHUNK - 8:CHUNK, :]
        conv = cb + cw[CONV_W - 1:CONV_W, :] * u
        for j in range(CONV_W - 1):
            lag = CONV_W - 1 - j
            conv = conv + cw[j:j + 1, :] * ext_s[8 - lag:8 - lag + CHUNK, :]
        qk = conv * _sigmoid(conv)
        q = qk[:, :hq]
        k = qk[:, hq:] * scale

        g = gc_ref[pl.ds(r0, CHUNK), :] + gb
        fc = _cumsum_rows(tril, _log_sigmoid(g))
        zpad = jnp.zeros((LANES - CHUNK, LANES), F32)
        g_t = jnp.concatenate([g, zpad], axis=0).T
        fc_t = jnp.concatenate([fc, zpad], axis=0).T

        for h in range(H_B):
            ig_c = g[:, h:h + 1]
            fc_c = fc[:, H_B + h:H_B + h + 1]
            ig_r = g_t[h:h + 1, 0:CHUNK]
            fc_r = fc_t[H_B + h:H_B + h + 1, 0:CHUNK]
            g_tot = fc_c[CHUNK - 1:CHUNK, :]
            qh = q[:, h * DQK_B:(h + 1) * DQK_B]
            kh = k[:, h * DQK_B:(h + 1) * DQK_B]
            vh = v_ref[pl.ds(r0, CHUNK), h * DV_B:(h + 1) * DV_B]
            qb = qh.astype(BF16)
            c_st = c_ref[h]
            n_st = n_ref[h]
            m_st = m_ref[h][0:1, 0:1]

            d_log = jnp.where(causal, fc_c - fc_r + ig_r, -jnp.inf)
            a_int = fc_c + m_st
            m_t = jnp.maximum(jnp.max(d_log, axis=-1, keepdims=True), a_int)
            w_in = jnp.exp(d_log - m_t)
            w_state = jnp.exp(a_int - m_t)
            qkw = _dot_nt(qb, kh.astype(BF16)) * w_in
            num = (jnp.dot(qkw.astype(BF16), vh, preferred_element_type=F32)
                   + w_state * jnp.dot(qb, c_st.astype(BF16), preferred_element_type=F32))
            den = (jnp.sum(qkw, axis=-1, keepdims=True)
                   + w_state * jnp.sum(qh * n_st, axis=-1, keepdims=True))
            hh = num / jnp.maximum(jnp.abs(den), jnp.exp(-m_t))

            a_end = g_tot - fc_c + ig_c
            m_loc = jnp.max(a_end, axis=0, keepdims=True)
            kw = kh * jnp.exp(a_end - m_loc)
            c_loc = _dot_tn(kw.astype(BF16), vh)
            n_loc = jnp.sum(kw, axis=0, keepdims=True)
            m_new = jnp.maximum(g_tot + m_st, m_loc)
            s_prev = jnp.exp(g_tot + m_st - m_new)
            s_loc = jnp.exp(m_loc - m_new)
            c_ref[h] = s_prev * c_st + s_loc * c_loc
            n_ref[h] = s_prev * n_st + s_loc * n_loc
            m_ref[h] = jnp.broadcast_to(m_new, m_ref.shape[1:])

            vs = slice(h * DV_B, (h + 1) * DV_B)
            ms = jnp.mean(hh * hh, axis=-1, keepdims=True)
            og = o_in_ref[pl.ds(r0, CHUNK), vs].astype(F32)
            y = hh * lax.rsqrt(ms + EPS) * gain[:, vs] * _sigmoid(og)
            o_ref[pl.ds(r0, CHUNK), vs] = y.astype(BF16)

    return chunk


def _mixers_kernel(q_ref, f_ref, va_ref, g_ref, lb_ref, gain_a_ref,
                   qk_ref, vb_ref, ob_ref, gc_ref, cw_ref, cb_ref, gb_ref, gain_b_ref, tril_ref,
                   ya_ref, yb_ref, st_ref, c_ref, n_ref, m_ref, tail_ref, ext_s):
    @pl.when(pl.program_id(1) == 0)
    def _():
        st_ref[...] = jnp.zeros_like(st_ref)
        c_ref[...] = jnp.zeros_like(c_ref)
        n_ref[...] = jnp.zeros_like(n_ref)
        m_ref[...] = jnp.full_like(m_ref, M_INIT)
        tail_ref[...] = jnp.zeros_like(tail_ref)

    hgrn_chunk = _hgrn_chunk_fn(q_ref, f_ref, va_ref, g_ref, lb_ref, gain_a_ref, tril_ref,
                                ya_ref, st_ref)
    mlstm_chunk = _mlstm_chunk_fn(qk_ref, vb_ref, ob_ref, gc_ref, cw_ref, cb_ref, gb_ref,
                                  gain_b_ref, tril_ref, yb_ref, c_ref, n_ref, m_ref, tail_ref,
                                  ext_s)

    def body(c, carry):
        hgrn_chunk(c)
        mlstm_chunk(c)
        return carry

    lax.fori_loop(0, q_ref.shape[0] // CHUNK, body, 0)


def _mixers(z3, gc3, lb_raw, gain_a, conv_w, conv_b, gate_bias, gain_b, tril, ts):
    b, s, _ = z3.shape
    w = H_A * DK_A
    assert w == H_B * DV_B == 2 * H_B * DQK_B, "column blocks of z share one width"

    def zspec(col):
        return pl.BlockSpec((None, ts, w), lambda i, j, col=col: (i, j, col))

    def const(shape):
        return pl.BlockSpec(shape, lambda i, j: (0,) * len(shape))

    out = jax.ShapeDtypeStruct((b, s, w), BF16)
    out_spec = pl.BlockSpec((None, ts, w), lambda i, j: (i, j, 0))
    return pl.pallas_call(
        _mixers_kernel,
        out_shape=(out, out),
        grid=(b, s // ts),
        in_specs=[zspec(0), zspec(1), zspec(2), zspec(3), const((2, w)), const((1, w)),
                  zspec(4), zspec(5), zspec(6),
                  pl.BlockSpec((None, ts, LANES), lambda i, j: (i, j, 0)),
                  const((CONV_W, w)), const((1, w)), const((1, LANES)), const((1, w)),
                  const((CHUNK, CHUNK))],
        out_specs=(out_spec, out_spec),
        scratch_shapes=[pltpu.VMEM((H_A, DV_A, DK_A), F32),
                        pltpu.VMEM((H_B, DQK_B, DV_B), F32),
                        pltpu.VMEM((H_B, 1, DQK_B), F32),
                        pltpu.VMEM((H_B, 8, LANES), F32),
                        pltpu.VMEM((8, w), F32),
                        pltpu.VMEM((8 + CHUNK, w), F32)],
        compiler_params=_cparams(("parallel", "arbitrary")),
        name="mixers",
    )(z3, z3, z3, z3, lb_raw, gain_a, z3, z3, z3, gc3, conv_w, conv_b, gate_bias, gain_b, tril)


def _merge_kernel(ya_ref, yb_ref, ra0_ref, ra1_ref, rb0_ref, rb1_ref, x_ref,
                  wa_ref, wb_ref, wo_ref, g_ref, rw_ref, rb_ref, upper_ref,
                  x1_ref, hp_ref, idx_ref, gate_ref, gcol_ref, rank_ref, cnt_ref, carry_ref):
    @pl.when(pl.program_id(0) == 0)
    def _():
        carry_ref[...] = jnp.zeros_like(carry_ref)

    half = wo_ref.shape[0] // 2
    pa = jnp.dot(ya_ref[...], wa_ref[...], preferred_element_type=F32)
    pb = jnp.dot(yb_ref[...], wb_ref[...], preferred_element_type=F32)
    mix0 = (_sigmoid(ra0_ref[...].astype(F32)) * pa[:, :half]
            + _sigmoid(rb0_ref[...].astype(F32)) * pb[:, :half]).astype(BF16)
    mix1 = (_sigmoid(ra1_ref[...].astype(F32)) * pa[:, half:]
            + _sigmoid(rb1_ref[...].astype(F32)) * pb[:, half:]).astype(BF16)
    x1 = (x_ref[...] + jnp.dot(mix0, wo_ref[0:half, :], preferred_element_type=F32)
          + jnp.dot(mix1, wo_ref[half:, :], preferred_element_type=F32))
    x1_ref[...] = x1

    ms = jnp.mean(x1 * x1, axis=-1, keepdims=True)
    h2 = x1 * lax.rsqrt(ms + EPS) * g_ref[...]
    hb = h2.astype(BF16)
    lo = pltpu.bitcast(hb[:, :half].astype(F32), U32) >> 16
    hi = pltpu.bitcast(hb[:, half:].astype(F32), U32) & jnp.uint32(0xFFFF0000)
    hp_ref[...] = lo | hi

    h_lo = (h2 - hb.astype(F32)).astype(BF16)
    rw = rw_ref[...]
    rw_hi = rw.astype(BF16)
    rw_lo = (rw - rw_hi.astype(F32)).astype(BF16)
    logits = (_dot_nt(rw_hi, hb) + _dot_nt(rw_hi, h_lo) + _dot_nt(rw_lo, hb)) + rb_ref[...]

    tm = logits.shape[1]
    erow = lax.broadcasted_iota(I32, (N_EXPERTS, tm), 0)
    work = logits
    vals, idxs = [], []
    for _ in range(TOP_K):
        mval = jnp.max(work, axis=0, keepdims=True)
        idx = jnp.min(jnp.where(work == mval, erow, N_EXPERTS), axis=0, keepdims=True)
        work = jnp.where(erow == idx, -jnp.inf, work)
        vals.append(mval)
        idxs.append(idx)
    exps = [jnp.exp(v - vals[0]) for v in vals]
    denom = exps[0] + exps[1] + exps[2] + exps[3]
    gates = [e / denom for e in exps]

    onehots = [(erow == idx).astype(F32) for idx in idxs]
    mask = onehots[0] + onehots[1] + onehots[2] + onehots[3]
    before = jnp.dot(mask.astype(BF16), upper_ref[...], preferred_element_type=F32)
    before = before + carry_ref[:, 0:1]
    carry = carry_ref[...] + jnp.sum(mask, axis=1, keepdims=True)
    carry_ref[...] = carry
    cnt_ref[...] = carry

    idx_ref[...] = jnp.concatenate(idxs, axis=0)
    gate_rows = jnp.concatenate(gates, axis=0)
    gate_ref[...] = gate_rows
    rank_ref[...] = jnp.concatenate(
        [jnp.sum(oh * before, axis=0, keepdims=True) for oh in onehots], axis=0).astype(I32)
    padded = jnp.concatenate([gate_rows, jnp.zeros((LANES - TOP_K, tm), F32)], axis=0)
    gcol_ref[...] = padded.T


def _merge(ya, yb, z, x2d, wa, wb, wo, gain, rw_t, rb_col, upper, tm):
    t, d = x2d.shape
    half = d // 2
    ra_blk = (4 * H_A * DK_A + 3 * H_B * DV_B) // half

    def rows(width, col=0):
        return pl.BlockSpec((tm, width), lambda i, col=col: (i, col))

    def const(shape):
        return pl.BlockSpec(shape, lambda i: (0,) * len(shape), pipeline_mode=pl.Buffered(1))

    out_shape = (
        jax.ShapeDtypeStruct((t, d), F32),
        jax.ShapeDtypeStruct((t, half), U32),
        jax.ShapeDtypeStruct((TOP_K, t), I32),
        jax.ShapeDtypeStruct((TOP_K, t), F32),
        jax.ShapeDtypeStruct((t, LANES), F32),
        jax.ShapeDtypeStruct((TOP_K, t), I32),
        jax.ShapeDtypeStruct((N_EXPERTS, LANES), F32),
    )
    return pl.pallas_call(
        _merge_kernel,
        out_shape=out_shape,
        grid=(t // tm,),
        in_specs=[rows(ya.shape[1]), rows(yb.shape[1]),
                  rows(half, ra_blk), rows(half, ra_blk + 1), rows(half, ra_blk + 2),
                  rows(half, ra_blk + 3), rows(d),
                  const(wa.shape), const(wb.shape), const(wo.shape), const((1, d)),
                  const(rw_t.shape), const((N_EXPERTS, 1)), const((tm, tm))],
        out_specs=(rows(d), rows(half),
                   pl.BlockSpec((TOP_K, tm), lambda i: (0, i)),
                   pl.BlockSpec((TOP_K, tm), lambda i: (0, i)),
                   rows(LANES),
                   pl.BlockSpec((TOP_K, tm), lambda i: (0, i)),
                   pl.BlockSpec((N_EXPERTS, LANES), lambda i: (0, 0))),
        scratch_shapes=[pltpu.VMEM((N_EXPERTS, LANES), F32)],
        compiler_params=_cparams(("arbitrary",)),
        name="merge_router",
    )(ya, yb, z, z, z, z, x2d, wa, wb, wo, gain, rw_t, rb_col, upper)


def _dispatch_kernel(pad_end_ref, padded_ref, pos_ref, src_ref, dst_ref, zero_s, zsem, sem):
    td = src_ref.shape[0]

    @pl.when(pl.program_id(0) == 0)
    def _():
        zero_s[...] = jnp.zeros_like(zero_s)

        def tail_copy(e):
            start = pl.multiple_of(pad_end_ref[e] - EXPERT_TILE, EXPERT_TILE)
            return pltpu.make_async_copy(zero_s, dst_ref.at[pl.ds(start, EXPERT_TILE)], zsem)

        for e in range(N_EXPERTS):
            @pl.when(padded_ref[e] > 0)
            def _():
                tail_copy(e).start()
        for e in range(N_EXPERTS):
            @pl.when(padded_ref[e] > 0)
            def _():
                tail_copy(e).wait()

        def clear_tile(j, carry):
            start = pl.multiple_of(j * EXPERT_TILE, EXPERT_TILE)
            cp = pltpu.make_async_copy(zero_s, dst_ref.at[pl.ds(start, EXPERT_TILE)], zsem)
            cp.start()
            cp.wait()
            return carry

        lax.fori_loop(pad_end_ref[N_EXPERTS - 1] // EXPERT_TILE,
                      dst_ref.shape[0] // EXPERT_TILE, clear_tile, 0)

    def issue(r, carry):
        for k in range(TOP_K):
            pltpu.make_async_copy(src_ref.at[pl.ds(r, 1)],
                                  dst_ref.at[pl.ds(pos_ref[k, r], 1)], sem).start(priority=k % 2)
        return carry

    lax.fori_loop(0, td, issue, 0, unroll=8)
    for k in range(TOP_K):
        pltpu.make_async_copy(src_ref, dst_ref.at[pl.ds(0, td)], sem).wait()


def _dispatch(pad_end, padded, pos, src, n_rows, td):
    t, d = src.shape
    grid_spec = pltpu.PrefetchScalarGridSpec(
        num_scalar_prefetch=2,
        grid=(t // td,),
        in_specs=[pl.BlockSpec((TOP_K, td), lambda i, pe, pd: (0, i), memory_space=pltpu.SMEM),
                  pl.BlockSpec((td, d), lambda i, pe, pd: (i, 0))],
        out_specs=pl.BlockSpec(memory_space=pl.ANY),
        scratch_shapes=[pltpu.VMEM((EXPERT_TILE, d), src.dtype),
                        pltpu.SemaphoreType.DMA, pltpu.SemaphoreType.DMA],
    )
    return pl.pallas_call(
        _dispatch_kernel,
        out_shape=jax.ShapeDtypeStruct((n_rows, d), src.dtype),
        grid_spec=grid_spec,
        compiler_params=pltpu.CompilerParams(dimension_semantics=("arbitrary",),
                                             vmem_limit_bytes=V7X_VMEM_LIMIT_BYTES,
                                             has_side_effects=True),
        name="dispatch",
    )(pad_end, padded, pos, src)


def _expert_kernel(te_ref, tv_ref, xs_ref, wgu_hbm, wdn_hbm, bgu_ref, bdn_ref, o_ref,
                   wgu_s, wdn_s, stage_s, sems, xb_s, act_s, cur_ref):
    i = pl.program_id(0)
    valid = tv_ref[i] > 0
    e = te_ref[i]
    d, two_ff = wgu_s.shape
    d_ff = two_ff // 2
    half = d // 2
    n_slots, rows_per, _ = stage_s.shape
    n_gu = 2 * (d // rows_per)
    n_all = n_gu + d_ff // rows_per

    @pl.when(i == 0)
    def _():
        cur_ref[0] = -1

    @pl.when(jnp.logical_and(valid, cur_ref[0] != e))
    def _():
        def piece(j):
            if j < n_gu:
                return j // 2, j % 2
            return j - n_gu, 0

        def chunk_copy(j):
            r, c = piece(j)
            rows = pl.ds(r * rows_per, rows_per)
            if j < n_gu:
                src = wgu_hbm.at[e, rows, pl.ds(c * d_ff, d_ff)]
            else:
                src = wdn_hbm.at[e, rows, :]
            return pltpu.make_async_copy(src, stage_s.at[j % n_slots], sems.at[j % n_slots])

        for j in range(n_slots - 1):
            chunk_copy(j).start()
        for j in range(n_all):
            if j + n_slots - 1 < n_all:
                chunk_copy(j + n_slots - 1).start()
            chunk_copy(j).wait()
            w = stage_s[j % n_slots].astype(BF16)
            r, c = piece(j)
            if j < n_gu:
                wgu_s[r * rows_per:(r + 1) * rows_per, c * d_ff:(c + 1) * d_ff] = w
            else:
                wdn_s[r * rows_per:(r + 1) * rows_per, :] = w
        cur_ref[0] = e

    @pl.when(valid)
    def _():
        w = xs_ref[...]
        xb_s[:, 0:half] = pltpu.bitcast(w << 16, F32).astype(BF16)
        xb_s[:, half:] = pltpu.bitcast(w & jnp.uint32(0xFFFF0000), F32).astype(BF16)
        xb = xb_s[...]
        for c in range(d_ff // FF_CHUNK):
            ca = slice(c * FF_CHUNK, (c + 1) * FF_CHUNK)
            cu = slice(d_ff + c * FF_CHUNK, d_ff + (c + 1) * FF_CHUNK)
            a = jnp.dot(xb, wgu_s[:, ca], preferred_element_type=F32) + bgu_ref[:, ca]
            u = jnp.dot(xb, wgu_s[:, cu], preferred_element_type=F32) + bgu_ref[:, cu]
            a = jnp.minimum(a, SWIGLU_LIMIT)
            u = jnp.clip(u, -SWIGLU_LIMIT, SWIGLU_LIMIT)
            act_s[:, ca] = ((u + 1.0) * a * _sigmoid(SWIGLU_ALPHA * a)).astype(BF16)
        o_ref[...] = jnp.dot(act_s[...], wdn_s[...], preferred_element_type=F32) + bdn_ref[...]

    @pl.when(jnp.logical_not(valid))
    def _():
        o_ref[...] = jnp.zeros_like(o_ref)


def _experts(tile_expert, tile_valid, xs, w_gu, b_gu, w_dn, b_dn):
    n_rows, half = xs.shape
    d = 2 * half
    d_ff = w_dn.shape[1]
    n_tiles = n_rows // EXPERT_TILE

    grid_spec = pltpu.PrefetchScalarGridSpec(
        num_scalar_prefetch=2,
        grid=(n_tiles,),
        in_specs=[
            pl.BlockSpec((EXPERT_TILE, half), lambda i, te, tv: (jnp.minimum(i, tv[n_tiles]), 0)),
            pl.BlockSpec(memory_space=pl.ANY),
            pl.BlockSpec(memory_space=pl.ANY),
            pl.BlockSpec((None, 1, 2 * d_ff), lambda i, te, tv: (te[i], 0, 0)),
            pl.BlockSpec((None, 1, d), lambda i, te, tv: (te[i], 0, 0)),
        ],
        out_specs=pl.BlockSpec((EXPERT_TILE, d), lambda i, te, tv: (i, 0)),
        scratch_shapes=[pltpu.VMEM((d, 2 * d_ff), BF16),
                        pltpu.VMEM((d_ff, d), BF16),
                        pltpu.VMEM((WEIGHT_RING, WEIGHT_ROWS, d_ff), F32),
                        pltpu.SemaphoreType.DMA((WEIGHT_RING,)),
                        pltpu.VMEM((EXPERT_TILE, d), BF16),
                        pltpu.VMEM((EXPERT_TILE, d_ff), BF16),
                        pltpu.SMEM((1,), I32)],
    )
    assert d == d_ff, "weight staging shares one (WEIGHT_ROWS, d_ff) buffer for w_gu and w_dn pieces"
    return pl.pallas_call(
        _expert_kernel,
        out_shape=jax.ShapeDtypeStruct((n_rows, d), F32),
        grid_spec=grid_spec,
        compiler_params=_cparams(("arbitrary",)),
        name="experts",
    )(tile_expert, tile_valid, xs, w_gu, w_dn, b_gu, b_dn)


def _combine_kernel(pos_ref, posn_ref, ys_ref, x1_ref, gcol_ref, p_ref, g_ple_ref, wpg_ref,
                    wple_ref, g_fin_ref, o_ref, buf, sems):
    tc = x1_ref.shape[0]
    i = pl.program_id(0)
    slot = lax.rem(i, 2)

    def gather(idx_ref, dst_slot):
        def issue(r, carry):
            for k in range(TOP_K):
                pltpu.make_async_copy(ys_ref.at[pl.ds(idx_ref[k, r], 1)],
                                      buf.at[dst_slot, k, pl.ds(r, 1)],
                                      sems.at[dst_slot]).start(priority=k % 2)
            return carry
        lax.fori_loop(0, tc, issue, 0, unroll=True)

    @pl.when(i == 0)
    def _():
        gather(pos_ref, slot)

    gather(posn_ref, 1 - slot)

    def wait_slot(s):
        for k in range(TOP_K):
            pltpu.make_async_copy(ys_ref.at[pl.ds(0, tc)], buf.at[s, k], sems.at[s]).wait()

    ple = jnp.dot(p_ref[...].astype(BF16), wple_ref[...], preferred_element_type=F32)
    wait_slot(slot)

    @pl.when(i + 1 == pl.num_programs(0))
    def _():
        wait_slot(1 - slot)

    gcol = gcol_ref[...]
    moe = buf[slot, 0] * gcol[:, 0:1]
    for k in range(1, TOP_K):
        moe = moe + buf[slot, k] * gcol[:, k:k + 1]
    x2 = x1_ref[...] + moe
    ms = jnp.mean(x2 * x2, axis=-1, keepdims=True)
    h3 = (x2 * lax.rsqrt(ms + EPS) * g_ple_ref[...]).astype(BF16)
    gate = _sigmoid(jnp.dot(h3, wpg_ref[...], preferred_element_type=F32))
    x3 = x2 + ple * gate
    ms3 = jnp.mean(x3 * x3, axis=-1, keepdims=True)
    o_ref[...] = x3 * lax.rsqrt(ms3 + EPS) * g_fin_ref[...]


def _combine(pos, ys, x1, gcol, p2d, g_ple, wpg, wple, g_fin, tc):
    t, d = x1.shape
    n_steps = t // tc

    def rows(width):
        return pl.BlockSpec((tc, width), lambda i: (i, 0))

    def const(shape):
        return pl.BlockSpec(shape, lambda i: (0,) * len(shape), pipeline_mode=pl.Buffered(1))

    return pl.pallas_call(
        _combine_kernel,
        out_shape=jax.ShapeDtypeStruct((t, d), F32),
        grid=(t // tc,),
        in_specs=[pl.BlockSpec((TOP_K, tc), lambda i: (0, i), memory_space=pltpu.SMEM),
                  pl.BlockSpec((TOP_K, tc), lambda i: (0, jnp.minimum(i + 1, n_steps - 1)),
                               memory_space=pltpu.SMEM),
                  pl.BlockSpec(memory_space=pl.ANY),
                  rows(d), rows(LANES), rows(p2d.shape[1]),
                  const((1, d)), const(wpg.shape), const(wple.shape), const((1, d))],
        out_specs=rows(d),
        scratch_shapes=[pltpu.VMEM((2, TOP_K, tc, d), F32), pltpu.SemaphoreType.DMA((2,))],
        compiler_params=_cparams(("arbitrary",)),
        name="combine_ple",
    )(pos, pos, ys, x1, gcol, p2d, g_ple, wpg, wple, g_fin)


def _pick(n, pref):
    return pref if n % pref == 0 else n


def kernel(x, p, norm_mix, w_in, hgrn_lb, hgrn_norm, mlstm_conv_w, mlstm_conv_b, mlstm_b_i,
           mlstm_b_f, mlstm_norm, w_branch_a, w_branch_b, w_out, norm_moe, router_w, router_b,
           exp_w_gu, exp_b_gu, exp_w_dn, exp_b_dn, norm_ple, w_ple, w_ple_gate, norm_final):
    b, s, d = x.shape
    t = b * s
    depth = norm_mix.shape[0]
    wa_cols = H_A * DK_A
    wb_cols = H_B * DV_B
    g0 = 4 * wa_cols + 2 * wb_cols
    g1 = g0 + 2 * H_B
    tril = jnp.tril(jnp.ones((CHUNK, CHUNK), F32)).astype(BF16)
    x2d = x.reshape(t, d)

    for i in range(depth):
        w_i = w_in[i]
        w_main = jnp.concatenate([w_i[:, :g0], w_i[:, g1:]], axis=1).astype(BF16)
        w_gate = jnp.pad(w_i[:, g0:g1], ((0, 0), (0, LANES - 2 * H_B))).astype(BF16)
        gate_bias = jnp.pad(jnp.concatenate([mlstm_b_i[i], mlstm_b_f[i]]),
                            (0, LANES - 2 * H_B)).reshape(1, LANES)

        tm_in = _pick(t, 1024)
        tn_in = _pick(w_main.shape[1], 1408)
        z, gc = _inproj(x2d, norm_mix[i].reshape(1, d), w_main, w_gate, tm_in, tn_in)
        z3 = z.reshape(b, s, z.shape[1])
        gc3 = gc.reshape(b, s, LANES)

        ts = _pick(s, 1024)
        ya, yb = _mixers(z3, gc3, hgrn_lb[i:i + 2], hgrn_norm[i].reshape(1, wa_cols),
                         mlstm_conv_w[i], mlstm_conv_b[i].reshape(1, -1), gate_bias,
                         mlstm_norm[i].reshape(1, wb_cols), tril, ts)

        tm_mg = _pick(t, 512)
        upper = jnp.triu(jnp.ones((tm_mg, tm_mg), F32), 1).astype(BF16)
        x1, hp, idx, gate_rows, gcol, rank, cnt = _merge(
            ya.reshape(t, wa_cols), yb.reshape(t, wb_cols), z, x2d,
            w_branch_a[i].astype(BF16), w_branch_b[i].astype(BF16), w_out[i].astype(BF16),
            norm_moe[i].reshape(1, d), router_w[i].T, router_b[i].reshape(N_EXPERTS, 1),
            upper, tm_mg)
        del gate_rows

        counts = cnt[:, 0].astype(I32)
        padded = (counts + EXPERT_TILE - 1) // EXPERT_TILE * EXPERT_TILE
        pad_end = jnp.cumsum(padded)
        pad_start = pad_end - padded
        e_ids = jnp.arange(N_EXPERTS, dtype=I32)
        pos = rank + jnp.sum(jnp.where(idx[..., None] == e_ids, pad_start, 0), axis=-1)
        n_rows = t * TOP_K + N_EXPERTS * EXPERT_TILE
        n_tiles = n_rows // EXPERT_TILE
        tile_row = jnp.arange(n_tiles, dtype=I32) * EXPERT_TILE
        n_valid = pad_end[-1] // EXPERT_TILE
        last_valid = jnp.maximum(n_valid - 1, 0)
        tile_valid = (tile_row < pad_end[-1]).astype(I32)
        tile_expert = jnp.sum((tile_row[:, None] >= pad_end[None, :]).astype(I32), axis=1)
        tile_expert = jnp.minimum(tile_expert, N_EXPERTS - 1)
        last_e = jnp.sum(jnp.where(jnp.arange(n_tiles) == last_valid, tile_expert, 0))
        tile_expert = jnp.where(tile_valid > 0, tile_expert, last_e).astype(I32)
        tile_valid = jnp.concatenate([tile_valid, last_valid.reshape(1).astype(I32)])

        xs = _dispatch(pad_end.astype(I32), padded, pos, hp, n_rows, _pick(t, 512))
        ys = _experts(tile_expert, tile_valid, xs,
                      exp_w_gu[i], exp_b_gu[i].reshape(N_EXPERTS, 1, -1),
                      exp_w_dn[i], exp_b_dn[i].reshape(N_EXPERTS, 1, d))

        is_last = i == depth - 1
        assert is_last, "the fused final-norm epilogue assumes a single layer"
        x2d = _combine(pos, ys, x1, gcol, p[i].reshape(t, -1), norm_ple[i].reshape(1, d),
                       w_ple_gate[i].astype(BF16), w_ple[i].astype(BF16),
                       norm_final.reshape(1, d), _pick(t, 256))
    return x2d.reshape(b, s, d)
```
